```python
import jax, jax.numpy as jnp
from jax import lax
import numpy as np


D_MODEL = 1024
BATCH = 4
SEQ = 8192
DEPTH = 1
DEC_BATCH = 128
DEC_SEQ = 8
PAST_LEN = 8192
PAGE_SIZE = 128

D_CONV = D_MODEL // 4
D_ATTN = D_MODEL - D_CONV
HEAD_DIM = 64
N_HEADS = D_ATTN // HEAD_DIM
D_IN = 2 * D_CONV + 3 * D_ATTN
CONV_A_WIDTH = 31
DILATED_GROUPS = ((128, 1), (512, 4), (2048, 16))
MAX_WINDOW = 2048
BAND_BLOCK = 128
D_FF = ((8 * D_MODEL // 3 + 255) // 256) * 256
FFN_CONV_WIDTH = 3
EPS = 1e-6
NEG_INF = -1e30
SCALE = HEAD_DIM ** -0.5

kernel_name = 'hymba_conformer_dilated_convffn_step'


def rms_norm(x, g):
    xf = x.astype(jnp.float32)
    y = xf * lax.rsqrt(jnp.mean(xf * xf, axis=-1, keepdims=True) + EPS)
    return (y * g.astype(jnp.float32)).astype(x.dtype)


def layer_norm(x, g, b):
    xf = x.astype(jnp.float32)
    mu = jnp.mean(xf, axis=-1, keepdims=True)
    xc = xf - mu
    y = xc * lax.rsqrt(jnp.mean(xc * xc, axis=-1, keepdims=True) + EPS)
    return (y * g.astype(jnp.float32) + b.astype(jnp.float32)).astype(x.dtype)


def depthwise_conv_valid(x_full, w, b):
    c = x_full.shape[-1]
    y = lax.conv_general_dilated(x_full, w[:, None, :], window_strides=(1,), padding='VALID',
                                 dimension_numbers=('NWC', 'WIO', 'NWC'), feature_group_count=c)
    return y + b


def softmax_attend(s, v, spec):
    mx = jnp.max(s, axis=-1, keepdims=True)
    p = jnp.exp(s - mx)
    den = jnp.sum(p, axis=-1)
    o = jnp.einsum(spec, p, v) / den[..., None]
    return o, mx[..., 0] + jnp.log(den)


def merge_dilation_groups(outs, lses, dtype):
    w = jax.nn.softmax(jnp.stack(lses), axis=0)
    return jnp.sum(w[..., None] * jnp.stack(outs), axis=0).astype(dtype)


def dilated_group_prompt(q, k, v, window, dilation):
    B, S, H, Dh = q.shape
    n_neigh = window // dilation
    span = dilation * BAND_BLOCK
    s_pad = -(-S // span) * span
    m_len = s_pad // dilation
    nb = m_len // BAND_BLOCK

    def to_sub(t):
        t = jnp.pad(t.astype(jnp.float32), ((0, 0), (0, s_pad - S), (0, 0), (0, 0)))
        t = t.reshape(B, m_len, dilation, H, Dh).transpose(0, 2, 1, 3, 4)
        return t.reshape(B, dilation, nb, BAND_BLOCK, H, Dh)

    def with_prev(t):
        prev = jnp.pad(t, ((0, 0), (0, 0), (1, 0), (0, 0), (0, 0), (0, 0)))[:, :, :-1]
        return jnp.concatenate([prev, t], axis=3)

    qs = to_sub(q)
    kk = with_prev(to_sub(k))
    vv = with_prev(to_sub(v))
    s = jnp.einsum('brnqhd,brnkhd->brnhqk', qs, kk) * SCALE
    qi = jnp.arange(BAND_BLOCK)[:, None]
    kc = jnp.arange(2 * BAND_BLOCK)[None, :]
    dist = qi + BAND_BLOCK - kc
    band = (dist >= 0) & (dist <= n_neigh)
    blk = jnp.arange(nb)[:, None, None]
    mask = band[None] & ((blk > 0) | (kc[None] >= BAND_BLOCK))
    s = jnp.where(mask[None, None, :, None], s, NEG_INF)
    o, lse = softmax_attend(s, vv, 'brnhqk,brnkhd->brnhqd')
    o = o.transpose(0, 2, 4, 1, 3, 5).reshape(B, s_pad, H, Dh)[:, :S]
    lse = lse.transpose(0, 2, 4, 1, 3).reshape(B, s_pad, H)[:, :S]
    return o, lse


def dilated_attention_prompt(q, k, v):
    res = [dilated_group_prompt(q, k, v, w, d) for (w, d) in DILATED_GROUPS]
    return merge_dilation_groups([r[0] for r in res], [r[1] for r in res], q.dtype)


def dilated_group_sample(q, kk, vv, window, dilation):
    T = q.shape[1]
    L = kk.shape[1] - T
    n_neigh = window // dilation
    idx = (L + jnp.arange(T))[:, None] - dilation * jnp.arange(n_neigh + 1)[None, :]
    valid = idx >= 0
    idx = jnp.maximum(idx, 0)
    kg = kk[:, idx]
    vg = vv[:, idx]
    s = jnp.einsum('bthd,btjhd->bthj', q, kg) * SCALE
    s = jnp.where(valid[None, :, None, :], s, NEG_INF)
    return softmax_attend(s, vg, 'bthj,btjhd->bthd')


def dilated_attention_sample(q, k, v, k_buf, v_buf):
    kk = jnp.concatenate([k_buf, k], axis=1).astype(jnp.float32)
    vv = jnp.concatenate([v_buf, v], axis=1).astype(jnp.float32)
    qf = q.astype(jnp.float32)
    res = [dilated_group_sample(qf, kk, vv, w, d) for (w, d) in DILATED_GROUPS]
    return merge_dilation_groups([r[0] for r in res], [r[1] for r in res], q.dtype)


def trunk_layer(x, conv_a_past, ffn_past, attend, norm_mix_g, w_in, conv_a_w, conv_a_b, ln_a_g, ln_a_b,
                w_out, norm_ffn_g, w_ffn_gate, w_ffn_up, ffn_conv_w, ffn_conv_b, w_ffn_down):
    B, S, _ = x.shape
    h = rms_norm(x, norm_mix_g)
    proj = h @ w_in
    a_val, a_gate, q, k, v = jnp.split(
        proj, [D_CONV, 2 * D_CONV, 2 * D_CONV + D_ATTN, 2 * D_CONV + 2 * D_ATTN], axis=-1)
    q = q.reshape(B, S, N_HEADS, HEAD_DIM)
    k = k.reshape(B, S, N_HEADS, HEAD_DIM)
    v = v.reshape(B, S, N_HEADS, HEAD_DIM)
    a_full = jnp.concatenate([conv_a_past, a_val * jax.nn.sigmoid(a_gate)], axis=1)
    a = jax.nn.silu(layer_norm(depthwise_conv_valid(a_full, conv_a_w, conv_a_b), ln_a_g, ln_a_b))
    o = attend(q, k, v).reshape(B, S, D_ATTN)
    x = x + jnp.concatenate([a, o], axis=-1) @ w_out
    h = rms_norm(x, norm_ffn_g)
    g_full = jnp.concatenate([ffn_past, h @ w_ffn_gate], axis=1)
    g = depthwise_conv_valid(g_full, ffn_conv_w, ffn_conv_b)
    x = x + (jax.nn.silu(g) * (h @ w_ffn_up)) @ w_ffn_down
    return x, k, v, a_full[:, -(CONV_A_WIDTH - 1):], g_full[:, -(FFN_CONV_WIDTH - 1):]


def setup_inputs(seed: int = 0) -> dict:
    key = jax.random.key(seed)
    ks = jax.random.split(key, 24)
    f32 = jnp.float32

    def nrm(k, shape, scale):
        return jax.random.normal(k, shape, f32) * scale

    win = min(MAX_WINDOW, PAST_LEN)
    return {
        'x_prompt': nrm(ks[0], (BATCH, SEQ, D_MODEL), 1.0),
        'x_sample': nrm(ks[1], (DEC_BATCH, DEC_SEQ, D_MODEL), 1.0),
        'cache_k_win': nrm(ks[2], (DEPTH, DEC_BATCH, win, N_HEADS, HEAD_DIM), 1.0),
        'cache_v_win': nrm(ks[3], (DEPTH, DEC_BATCH, win, N_HEADS, HEAD_DIM), 1.0),
        'state_conv_a': nrm(ks[4], (DEPTH, DEC_BATCH, CONV_A_WIDTH - 1, D_CONV), 0.5),
        'state_conv_ffn': nrm(ks[5], (DEPTH, DEC_BATCH, FFN_CONV_WIDTH - 1, D_FF), 1.0),
        'norm_mix_g': 1.0 + nrm(ks[6], (DEPTH, D_MODEL), 0.1),
        'w_in': nrm(ks[7], (DEPTH, D_MODEL, D_IN), D_MODEL ** -0.5),
        'conv_a_w': nrm(ks[8], (DEPTH, CONV_A_WIDTH, D_CONV), CONV_A_WIDTH ** -0.5),
        'conv_a_b': nrm(ks[9], (DEPTH, D_CONV), 0.02),
        'ln_a_g': 1.0 + nrm(ks[10], (DEPTH, D_CONV), 0.1),
        'ln_a_b': nrm(ks[11], (DEPTH, D_CONV), 0.02),
        'w_out': nrm(ks[12], (DEPTH, D_CONV + D_ATTN, D_MODEL), (D_CONV + D_ATTN) ** -0.5),
        'norm_ffn_g': 1.0 + nrm(ks[13], (DEPTH, D_MODEL), 0.1),
        'w_ffn_gate': nrm(ks[14], (DEPTH, D_MODEL, D_FF), D_MODEL ** -0.5),
        'w_ffn_up': nrm(ks[15], (DEPTH, D_MODEL, D_FF), D_MODEL ** -0.5),
        'ffn_conv_w': nrm(ks[16], (DEPTH, FFN_CONV_WIDTH, D_FF), FFN_CONV_WIDTH ** -0.5),
        'ffn_conv_b': nrm(ks[17], (DEPTH, D_FF), 0.02),
        'w_ffn_down': nrm(ks[18], (DEPTH, D_FF, D_MODEL), D_FF ** -0.5),
        'norm_final_g': 1.0 + nrm(ks[19], (D_MODEL,), 0.1),
    }


def reference(x_prompt, x_sample, cache_k_win, cache_v_win, state_conv_a, state_conv_ffn,
              norm_mix_g, w_in, conv_a_w, conv_a_b, ln_a_g, ln_a_b, w_out,
              norm_ffn_g, w_ffn_gate, w_ffn_up, ffn_conv_w, ffn_conv_b, w_ffn_down, norm_final_g):
    B, S, _ = x_prompt.shape
    win_p = min(MAX_WINDOW, S)
    xp, xs = x_prompt, x_sample
    kp, vp, cap, cfp = [], [], [], []
    ksn, vsn, cas, cfs = [], [], [], []
    for l in range(DEPTH):
        params = (norm_mix_g[l], w_in[l], conv_a_w[l], conv_a_b[l], ln_a_g[l], ln_a_b[l], w_out[l],
                  norm_ffn_g[l], w_ffn_gate[l], w_ffn_up[l], ffn_conv_w[l], ffn_conv_b[l], w_ffn_down[l])
        zeros_a = jnp.zeros((B, CONV_A_WIDTH - 1, D_CONV), xp.dtype)
        zeros_f = jnp.zeros((B, FFN_CONV_WIDTH - 1, D_FF), xp.dtype)
        xp, k, v, ca, cf = trunk_layer(xp, zeros_a, zeros_f, dilated_attention_prompt, *params)
        kp.append(k[:, S - win_p:])
        vp.append(v[:, S - win_p:])
        cap.append(ca)
        cfp.append(cf)
        k_buf, v_buf = cache_k_win[l], cache_v_win[l]

        def attend_sample(q, k_new, v_new, k_buf=k_buf, v_buf=v_buf):
            return dilated_attention_sample(q, k_new, v_new, k_buf, v_buf)

        xs, k, v, ca, cf = trunk_layer(xs, state_conv_a[l], state_conv_ffn[l], attend_sample, *params)
        ksn.append(k)
        vsn.append(v)
        cas.append(ca)
        cfs.append(cf)
    y_prompt = rms_norm(xp, norm_final_g)
    y_sample = rms_norm(xs, norm_final_g)
    return (y_prompt, y_sample, jnp.stack(kp), jnp.stack(vp), jnp.stack(cap), jnp.stack(cfp),
            jnp.stack(ksn), jnp.stack(vsn), jnp.stack(cas), jnp.stack(cfs))
```

```python
import functools

import jax
import jax.numpy as jnp
from jax import lax
from jax.experimental import pallas as pl
from jax.experimental.pallas import tpu as pltpu

D_MODEL = 1024
D_CONV = 256
D_ATTN = 768
HEAD_DIM = 64
N_HEADS = 12
D_IN = 2 * D_CONV + 3 * D_ATTN
CONV_A_WIDTH = 31
D_FF = 2816
EPS = 1e-6
NEG_INF = -1e30
SCALE = HEAD_DIM ** -0.5

DILATIONS = (1, 4, 16)
NEIGHBOURS = 128
QBLOCK = 128
SPAN = DILATIONS[-1] * QBLOCK
CACHE_LEN = 2048
DEC_SEQ = 8

LANES = 128
FF_CHUNK = 256
N_FF_CHUNKS = D_FF // FF_CHUNK
ROW_TILE = 512
VMEM_LIMIT = 56 * 1024 * 1024

F32 = jnp.float32
BF16 = jnp.bfloat16


def _const_spec(shape):
    nd = len(shape)
    return pl.BlockSpec(shape, lambda *_: (0,) * nd, pipeline_mode=pl.Buffered(1))


def _params(n_axes):
    return pltpu.CompilerParams(
        dimension_semantics=("arbitrary",) * n_axes, vmem_limit_bytes=VMEM_LIMIT)


def _rms_norm(x, g):
    ms = jnp.mean(x * x, axis=-1, keepdims=True)
    return x * lax.rsqrt(ms + EPS) * g


def _dot(a, b):
    return jnp.dot(a, b, preferred_element_type=F32)


def _dot_nt(a, b):
    return lax.dot_general(a, b, (((1,), (1,)), ((), ())), preferred_element_type=F32)


def _in_proj_kernel(x_ref, g_ref, w_ref, glu_ref, q_ref, k_ref, v_ref):
    h = _rms_norm(x_ref[...], g_ref[...]).astype(BF16)

    def proj(lo, hi):
        return _dot(h, w_ref[:, lo:hi])

    a_val = proj(0, D_CONV)
    a_gate = proj(D_CONV, 2 * D_CONV)
    glu_ref[...] = a_val * jax.nn.sigmoid(a_gate)
    q_ref[...] = proj(2 * D_CONV, 2 * D_CONV + D_ATTN)
    k_ref[...] = proj(2 * D_CONV + D_ATTN, 2 * D_CONV + 2 * D_ATTN)
    v_ref[...] = proj(2 * D_CONV + 2 * D_ATTN, D_IN)


def _in_proj(x2d, g, w_b):
    rows = x2d.shape[0]
    row = lambda width: pl.BlockSpec((ROW_TILE, width), lambda i: (i, 0))
    return pl.pallas_call(
        _in_proj_kernel,
        grid=(rows // ROW_TILE,),
        in_specs=[row(D_MODEL), _const_spec((1, D_MODEL)), _const_spec((D_MODEL, D_IN))],
        out_specs=[row(D_CONV), row(D_ATTN), row(D_ATTN), row(D_ATTN)],
        out_shape=[jax.ShapeDtypeStruct((rows, D_CONV), F32)]
        + [jax.ShapeDtypeStruct((rows, D_ATTN), F32)] * 3,
        compiler_params=_params(1),
        name="in_proj",
    )(x2d, g, w_b)


def _layer_norm_silu(y, g, b):
    mu = jnp.mean(y, axis=-1, keepdims=True)
    yc = y - mu
    z = yc * lax.rsqrt(jnp.mean(yc * yc, axis=-1, keepdims=True) + EPS) * g + b
    return z * jax.nn.sigmoid(z)


CONV_HALO = 32
CONV_ROWS = 64


def _conv_a_prompt_kernel(glu_ref, w_ref, b_ref, lg_ref, lb_ref, a_ref, win_ref):
    @pl.when(pl.program_id(1) == 0)
    def _():
        win_ref[0:CONV_HALO, :] = jnp.zeros((CONV_HALO, D_CONV), F32)

    @pl.when(pl.program_id(1) > 0)
    def _():
        win_ref[0:CONV_HALO, :] = win_ref[ROW_TILE:ROW_TILE + CONV_HALO, :]

    win_ref[CONV_HALO:, :] = glu_ref[...]
    first = CONV_HALO - (CONV_A_WIDTH - 1)
    for c in range(ROW_TILE // CONV_ROWS):
        r0 = c * CONV_ROWS
        acc = jnp.broadcast_to(b_ref[...], (CONV_ROWS, D_CONV))
        for j in range(CONV_A_WIDTH):
            acc = acc + w_ref[j:j + 1, :] * win_ref[r0 + first + j:r0 + first + j + CONV_ROWS, :]
        a_ref[r0:r0 + CONV_ROWS, :] = _layer_norm_silu(acc, lg_ref[...], lb_ref[...])


def _conv_a_prompt(glu, w, b, lg, lb):
    batch, seq, _ = glu.shape
    blk = pl.BlockSpec((None, ROW_TILE, D_CONV), lambda i, j: (i, j, 0))
    return pl.pallas_call(
        _conv_a_prompt_kernel,
        grid=(batch, seq // ROW_TILE),
        in_specs=[blk, _const_spec((CONV_A_WIDTH, D_CONV))] + [_const_spec((1, D_CONV))] * 3,
        out_specs=blk,
        out_shape=jax.ShapeDtypeStruct((batch, seq, D_CONV), F32),
        scratch_shapes=[pltpu.VMEM((CONV_HALO + ROW_TILE, D_CONV), F32)],
        compiler_params=_params(2),
        name="conv_a_prompt",
    )(glu, w, b, lg, lb)


def _conv_a_sample_kernel(past_ref, glu_ref, w_ref, b_ref, lg_ref, lb_ref, a_ref, state_ref):
    n_past = CONV_A_WIDTH - 1

    def row(u):
        return past_ref[u] if u < n_past else glu_ref[u - n_past]

    for t in range(DEC_SEQ):
        acc = jnp.broadcast_to(b_ref[...], a_ref.shape[1:])
        for j in range(CONV_A_WIDTH):
            acc = acc + w_ref[j:j + 1, :] * row(t + j)
        a_ref[t] = _layer_norm_silu(acc, lg_ref[...], lb_ref[...])
    for u in range(n_past):
        state_ref[u] = row(u + DEC_SEQ)


def _conv_a_sample(past_t, glu_t, w, b, lg, lb):
    n_past, nseq, _ = past_t.shape
    return pl.pallas_call(
        _conv_a_sample_kernel,
        out_shape=[jax.ShapeDtypeStruct((DEC_SEQ, nseq, D_CONV), F32),
                   jax.ShapeDtypeStruct((n_past, nseq, D_CONV), F32)],
        compiler_params=pltpu.CompilerParams(vmem_limit_bytes=VMEM_LIMIT),
        name="conv_a_sample",
    )(past_t, glu_t, w, b, lg, lb)


def _group_geometry(d):
    nb = SPAN // (d * QBLOCK)
    slot = (nb + 1) * QBLOCK
    return nb, slot


def _attn_prompt_kernel(q_ref, k_ref, v_ref, o_ref, *scratch):
    n_g = len(DILATIONS)
    qg = scratch[0:n_g]
    kg = scratch[n_g:2 * n_g]
    vg = scratch[2 * n_g:3 * n_g]
    acc_n = scratch[3 * n_g:4 * n_g]
    den_n = scratch[4 * n_g:5 * n_g]
    max_n = scratch[5 * n_g:6 * n_g]
    sp = pl.program_id(2)
    lane = lax.broadcasted_iota(jnp.int32, (QBLOCK, LANES), 1)
    head0 = lane < HEAD_DIM

    @pl.when(sp == 0)
    def _():
        for g, d in enumerate(DILATIONS):
            _, slot = _group_geometry(d)
            for r in range(d):
                kg[g][r * slot:r * slot + QBLOCK, :] = jnp.zeros((QBLOCK, LANES), BF16)
                vg[g][r * slot:r * slot + QBLOCK, :] = jnp.zeros((QBLOCK, 2 * LANES), BF16)

    @pl.when(sp > 0)
    def _():
        for g, d in enumerate(DILATIONS):
            nb, slot = _group_geometry(d)
            for r in range(d):
                lo = r * slot
                kg[g][lo:lo + QBLOCK, :] = kg[g][lo + nb * QBLOCK:lo + slot, :]
                vg[g][lo:lo + QBLOCK, :] = vg[g][lo + nb * QBLOCK:lo + slot, :]

    for g, d in enumerate(DILATIONS):
        nb, slot = _group_geometry(d)
        n = SPAN // d
        lane_n = lax.broadcasted_iota(jnp.int32, (n, LANES), 1)
        for r in range(d):
            rows = pl.ds(r, n, stride=d) if d > 1 else pl.ds(0, n)
            qv = q_ref[rows, :] * SCALE
            qg[g][0, r * n:(r + 1) * n, :] = jnp.where(lane_n < HEAD_DIM, qv, 0.0).astype(BF16)
            qg[g][1, r * n:(r + 1) * n, :] = jnp.where(lane_n < HEAD_DIM, 0.0, qv).astype(BF16)
            lo = r * slot + QBLOCK
            kg[g][lo:lo + n, :] = k_ref[rows, :].astype(BF16)
            vg[g][lo:lo + n, 0:LANES] = v_ref[rows, :].astype(BF16)
            vg[g][lo:lo + n, LANES:] = jnp.ones((n, LANES), BF16)

    qi = lax.broadcasted_iota(jnp.int32, (QBLOCK, 2 * QBLOCK), 0)
    kc = lax.broadcasted_iota(jnp.int32, (QBLOCK, 2 * QBLOCK), 1)
    band = (kc >= qi) & (kc <= qi + NEIGHBOURS)

    def unit(u, carry):
        q0 = pl.multiple_of(u * QBLOCK, QBLOCK)
        for g, d in enumerate(DILATIONS):
            nb, slot = _group_geometry(d)
            if nb == 1:
                r, blk = u, 0
            else:
                shift = nb.bit_length() - 1
                r, blk = lax.shift_right_logical(u, shift), lax.bitwise_and(u, nb - 1)
            k0 = pl.multiple_of((u + r) * QBLOCK, QBLOCK)
            kb = kg[g][pl.ds(k0, 2 * QBLOCK), :]
            vb = vg[g][pl.ds(k0, 2 * QBLOCK), :]
            first_key = jnp.where(sp * nb + blk == 0, QBLOCK, 0)
            mask = band & (kc >= first_key)
            parts = []
            for h in range(2):
                s = _dot_nt(qg[g][h, pl.ds(q0, QBLOCK), :], kb)
                s = jnp.where(mask, s, NEG_INF)
                m = jnp.max(s, axis=-1, keepdims=True)
                p = jnp.exp(s - m).astype(BF16)
                parts.append((_dot(p, vb), m))
            (e0, m0), (e1, m1) = parts
            acc = jnp.where(head0, e0[:, :LANES], e1[:, :LANES])
            den = jnp.where(head0, e0[:, LANES:], e1[:, LANES:])
            mx = jnp.where(head0, m0, m1)
            if d == 1:
                rows = pl.ds(q0, QBLOCK)
            else:
                rows = pl.ds(blk * (QBLOCK * d) + r, QBLOCK, stride=d)
            acc_n[g][rows, :] = acc
            den_n[g][rows, :] = den
            max_n[g][rows, :] = mx
        return carry

    lax.fori_loop(0, SPAN // QBLOCK, unit, 0)

    def merge(c, carry):
        rows = pl.ds(pl.multiple_of(c * QBLOCK, QBLOCK), QBLOCK)
        ms = [max_n[g][rows, :] for g in range(n_g)]
        m_all = functools.reduce(jnp.maximum, ms)
        ws = [jnp.exp(m - m_all) for m in ms]
        num = sum(w * acc_n[g][rows, :] for g, w in enumerate(ws))
        den = sum(w * den_n[g][rows, :] for g, w in enumerate(ws))
        o_ref[rows, :] = num / den
        return carry

    lax.fori_loop(0, SPAN // QBLOCK, merge, 0)


def _attn_prompt(q, k, v):
    batch, seq, _ = q.shape
    blk = pl.BlockSpec((None, SPAN, LANES), lambda b, hp, s: (b, s, hp))
    scratch = [pltpu.VMEM((2, SPAN, LANES), BF16) for _ in DILATIONS]
    scratch += [pltpu.VMEM((d * _group_geometry(d)[1], LANES), BF16) for d in DILATIONS]
    scratch += [pltpu.VMEM((d * _group_geometry(d)[1], 2 * LANES), BF16) for d in DILATIONS]
    scratch += [pltpu.VMEM((SPAN, LANES), F32) for _ in range(3 * len(DILATIONS))]
    return pl.pallas_call(
        _attn_prompt_kernel,
        grid=(batch, D_ATTN // LANES, seq // SPAN),
        in_specs=[blk, blk, blk],
        out_specs=blk,
        out_shape=jax.ShapeDtypeStruct((batch, seq, D_ATTN), F32),
        scratch_shapes=scratch,
        compiler_params=_params(3),
        name="attn_prompt",
    )(q, k, v)


HEADS_PER_STEP = 6


def _group_count(dist):
    total = None
    for d in DILATIONS:
        sh = d.bit_length() - 1
        hit = ((dist & (d - 1)) == 0) & ((dist >> sh) <= NEIGHBOURS) & (dist >= 0)
        total = hit.astype(F32) if total is None else total + hit.astype(F32)
    return total


def _attn_sample_kernel(q_ref, kn_ref, vn_ref, kc_ref, vc_ref, o_ref):
    qi = lax.broadcasted_iota(jnp.int32, (DEC_SEQ, CACHE_LEN), 0)
    pos = lax.broadcasted_iota(jnp.int32, (DEC_SEQ, CACHE_LEN), 1)
    cnt_c = _group_count(CACHE_LEN + qi - pos)
    qn = lax.broadcasted_iota(jnp.int32, (DEC_SEQ, DEC_SEQ), 0)
    kn = lax.broadcasted_iota(jnp.int32, (DEC_SEQ, DEC_SEQ), 1)
    cnt_n = _group_count(qn - kn)
    for h in range(HEADS_PER_STEP):
        qh = (q_ref[h] * SCALE).astype(BF16)
        s_c = jnp.where(cnt_c > 0, _dot(qh, kc_ref[h].astype(BF16)), NEG_INF)
        s_n = jnp.where(cnt_n > 0, _dot_nt(qh, kn_ref[h].astype(BF16)), NEG_INF)
        m = jnp.maximum(jnp.max(s_c, axis=-1, keepdims=True), jnp.max(s_n, axis=-1, keepdims=True))
        w_c = cnt_c * jnp.exp(s_c - m)
        w_n = cnt_n * jnp.exp(s_n - m)
        den = jnp.sum(w_c, axis=-1, keepdims=True) + jnp.sum(w_n, axis=-1, keepdims=True)
        num = _dot_nt(w_c.astype(BF16), vc_ref[h].astype(BF16)) + _dot(w_n.astype(BF16), vn_ref[h].astype(BF16))
        o_ref[h] = num / den


def _attn_sample(q4, kn4, vn4, kc_t, vc_t):
    nseq = q4.shape[0]
    new = pl.BlockSpec((None, HEADS_PER_STEP, DEC_SEQ, HEAD_DIM), lambda s, h: (s, h, 0, 0))
    cache = pl.BlockSpec((None, HEADS_PER_STEP, HEAD_DIM, CACHE_LEN), lambda s, h: (s, h, 0, 0))
    return pl.pallas_call(
        _attn_sample_kernel,
        grid=(nseq, N_HEADS // HEADS_PER_STEP),
        in_specs=[new, new, new, cache, cache],
        out_specs=new,
        out_shape=jax.ShapeDtypeStruct((nseq, N_HEADS, DEC_SEQ, HEAD_DIM), F32),
        compiler_params=_params(2),
        name="attn_sample",
    )(q4, kn4, vn4, kc_t, vc_t)


def _out_proj_kernel(x_ref, a_ref, o_ref, w_ref, x1_ref):
    y = _dot(a_ref[...].astype(BF16), w_ref[0:D_CONV, :])
    y = y + _dot(o_ref[...].astype(BF16), w_ref[D_CONV:, :])
    x1_ref[...] = x_ref[...] + y


def _out_proj(x2d, a2d, o2d, w_b):
    rows = x2d.shape[0]
    row = lambda width: pl.BlockSpec((ROW_TILE, width), lambda i: (i, 0))
    return pl.pallas_call(
        _out_proj_kernel,
        grid=(rows // ROW_TILE,),
        in_specs=[row(D_MODEL), row(D_CONV), row(D_ATTN), _const_spec((D_MODEL, D_MODEL))],
        out_specs=row(D_MODEL),
        out_shape=jax.ShapeDtypeStruct((rows, D_MODEL), F32),
        compiler_params=_params(1),
        name="out_proj",
    )(x2d, a2d, o2d, w_b)


FFN_HALO = 8


def _ffn_chunks(h_ref, acc_ref, wg_ref, wu_ref, cw_ref, wd_ref, conv):
    acc_ref[...] = jnp.zeros(acc_ref.shape, F32)

    def chunk(c, carry):
        h = h_ref[...]
        gp = _dot(h, wg_ref[c])
        cw = cw_ref[c]
        g = conv(c, gp, cw)
        act = (g * jax.nn.sigmoid(g)) * _dot(h, wu_ref[c])
        acc_ref[...] += _dot(act.astype(BF16), wd_ref[c])
        return carry

    lax.fori_loop(0, N_FF_CHUNKS, chunk, 0)


def _ffn_prompt_kernel(x1_ref, g2_ref, wg_ref, wu_ref, cw_ref, wd_ref, gf_ref, y_ref, st_ref,
                       h_ref, acc_ref, gbuf_ref, carry_ref):
    @pl.when(pl.program_id(1) == 0)
    def _():
        carry_ref[...] = jnp.zeros(carry_ref.shape, F32)

    h_ref[...] = _rms_norm(x1_ref[...], g2_ref[...]).astype(BF16)

    def conv(c, gp, cw):
        gbuf_ref[0:FFN_HALO, :] = carry_ref[c]
        gbuf_ref[FFN_HALO:, :] = gp
        tail = gp[ROW_TILE - FFN_HALO:, :]
        carry_ref[c] = tail
        st_ref[c] = tail
        return (cw[2:3, :] * gp + cw[1:2, :] * gbuf_ref[FFN_HALO - 1:FFN_HALO - 1 + ROW_TILE, :]
                + cw[0:1, :] * gbuf_ref[FFN_HALO - 2:FFN_HALO - 2 + ROW_TILE, :] + cw[3:4, :])

    _ffn_chunks(h_ref, acc_ref, wg_ref, wu_ref, cw_ref, wd_ref, conv)
    y_ref[...] = _rms_norm(x1_ref[...] + acc_ref[...], gf_ref[...])


def _ffn_weight_specs():
    return [_const_spec((1, D_MODEL)),
            _const_spec((N_FF_CHUNKS, D_MODEL, FF_CHUNK)), _const_spec((N_FF_CHUNKS, D_MODEL, FF_CHUNK)),
            _const_spec((N_FF_CHUNKS, 8, FF_CHUNK)), _const_spec((N_FF_CHUNKS, FF_CHUNK, D_MODEL)),
            _const_spec((1, D_MODEL))]


def _ffn_prompt(x1, g2, wg3, wu3, cw3, wd3, gf):
    batch, seq, _ = x1.shape
    blk = pl.BlockSpec((None, ROW_TILE, D_MODEL), lambda b, j: (b, j, 0))
    st = pl.BlockSpec((None, N_FF_CHUNKS, FFN_HALO, FF_CHUNK), lambda b, j: (b, 0, 0, 0))
    return pl.pallas_call(
        _ffn_prompt_kernel,
        grid=(batch, seq // ROW_TILE),
        in_specs=[blk] + _ffn_weight_specs(),
        out_specs=[blk, st],
        out_shape=[jax.ShapeDtypeStruct((batch, seq, D_MODEL), F32),
                   jax.ShapeDtypeStruct((batch, N_FF_CHUNKS, FFN_HALO, FF_CHUNK), F32)],
        scratch_shapes=[pltpu.VMEM((ROW_TILE, D_MODEL), BF16), pltpu.VMEM((ROW_TILE, D_MODEL), F32),
                        pltpu.VMEM((FFN_HALO + ROW_TILE, FF_CHUNK), F32),
                        pltpu.VMEM((N_FF_CHUNKS, FFN_HALO, FF_CHUNK), F32)],
        compiler_params=_params(2),
        name="ffn_prompt",
    )(x1, g2, wg3, wu3, cw3, wd3, gf)


def _ffn_sample_kernel(x1_ref, past_ref, g2_ref, wg_ref, wu_ref, cw_ref, wd_ref, gf_ref, y_ref, st_ref,
                       h_ref, acc_ref, gbuf_ref):
    rows = x1_ref.shape[0]
    nseq = rows // DEC_SEQ
    h_ref[...] = _rms_norm(x1_ref[...], g2_ref[...]).astype(BF16)

    def conv(c, gp, cw):
        gbuf_ref[0:2 * nseq, :] = past_ref[c]
        gbuf_ref[2 * nseq:, :] = gp
        st_ref[c] = gp[rows - 2 * nseq:, :]
        return (cw[2:3, :] * gp + cw[1:2, :] * gbuf_ref[nseq:nseq + rows, :]
                + cw[0:1, :] * gbuf_ref[0:rows, :] + cw[3:4, :])

    _ffn_chunks(h_ref, acc_ref, wg_ref, wu_ref, cw_ref, wd_ref, conv)
    y_ref[...] = _rms_norm(x1_ref[...] + acc_ref[...], gf_ref[...])


def _ffn_sample(x1, past3, g2, wg3, wu3, cw3, wd3, gf):
    rows = x1.shape[0]
    nseq = rows // DEC_SEQ
    return pl.pallas_call(
        _ffn_sample_kernel,
        out_shape=[jax.ShapeDtypeStruct((rows, D_MODEL), F32),
                   jax.ShapeDtypeStruct((N_FF_CHUNKS, 2 * nseq, FF_CHUNK), F32)],
        scratch_shapes=[pltpu.VMEM((rows, D_MODEL), BF16), pltpu.VMEM((rows, D_MODEL), F32),
                        pltpu.VMEM((2 * nseq + rows, FF_CHUNK), F32)],
        compiler_params=pltpu.CompilerParams(vmem_limit_bytes=VMEM_LIMIT),
        name="ffn_sample",
    )(x1, past3, g2, wg3, wu3, cw3, wd3, gf)


def _chunk_cols(w2d):
    return w2d.reshape(w2d.shape[0], N_FF_CHUNKS, FF_CHUNK).transpose(1, 0, 2)


def _unchunk_cols(w3d):
    return w3d.transpose(1, 0, 2).reshape(w3d.shape[1], D_FF)


def kernel(x_prompt, x_sample, cache_k_win, cache_v_win, state_conv_a, state_conv_ffn, norm_mix_g, w_in, conv_a_w, conv_a_b, ln_a_g, ln_a_b, w_out, norm_ffn_g, w_ffn_gate, w_ffn_up, ffn_conv_w, ffn_conv_b, w_ffn_down, norm_final_g):
    batch, seq, _ = x_prompt.shape
    nseq, dec_seq, _ = x_sample.shape
    assert w_in.shape[0] == 1, "single trunk layer"
    assert dec_seq == DEC_SEQ and cache_k_win.shape[2] == CACHE_LEN
    assert seq % SPAN == 0 and seq >= CACHE_LEN and (nseq * dec_seq) % ROW_TILE == 0

    g1 = norm_mix_g
    w_in_b = w_in[0].astype(BF16)
    w_out_b = w_out[0].astype(BF16)
    ca_w, ca_b, lg, lb = conv_a_w[0], conv_a_b, ln_a_g, ln_a_b
    wg3 = _chunk_cols(w_ffn_gate[0].astype(BF16))
    wu3 = _chunk_cols(w_ffn_up[0].astype(BF16))
    wd3 = w_ffn_down[0].astype(BF16).reshape(N_FF_CHUNKS, FF_CHUNK, D_MODEL)
    cw3 = _chunk_cols(jnp.concatenate(
        [ffn_conv_w[0], ffn_conv_b, jnp.zeros((4, D_FF), F32)], axis=0))
    gf = norm_final_g[None, :]

    xp = x_prompt.reshape(batch * seq, D_MODEL)
    glu_p, q_p, k_p, v_p = _in_proj(xp, g1, w_in_b)
    glu_p3 = glu_p.reshape(batch, seq, D_CONV)
    a_p = _conv_a_prompt(glu_p3, ca_w, ca_b, lg, lb)
    to3 = lambda t: t.reshape(batch, seq, D_ATTN)
    o_p = _attn_prompt(to3(q_p), to3(k_p), to3(v_p))
    x1_p = _out_proj(xp, a_p.reshape(batch * seq, D_CONV), o_p.reshape(batch * seq, D_ATTN), w_out_b)
    y_p, st_p = _ffn_prompt(x1_p.reshape(batch, seq, D_MODEL), norm_ffn_g, wg3, wu3, cw3, wd3, gf)

    win = CACHE_LEN
    heads = lambda t: t.reshape(batch, seq, N_HEADS, HEAD_DIM)[:, seq - win:][None]
    conv_a_prompt = glu_p3[:, seq - (CONV_A_WIDTH - 1):][None]
    conv_ffn_prompt = st_p[:, :, FFN_HALO - 2:, :].transpose(0, 2, 1, 3).reshape(batch, 2, D_FF)[None]

    xs = x_sample.transpose(1, 0, 2).reshape(dec_seq * nseq, D_MODEL)
    glu_s, q_s, k_s, v_s = _in_proj(xs, g1, w_in_b)
    a_s, st_a = _conv_a_sample(state_conv_a[0].transpose(1, 0, 2),
                               glu_s.reshape(dec_seq, nseq, D_CONV), ca_w, ca_b, lg, lb)
    per_head = lambda t: t.reshape(dec_seq, nseq, N_HEADS, HEAD_DIM).transpose(1, 2, 0, 3)
    o4 = _attn_sample(per_head(q_s), per_head(k_s), per_head(v_s),
                      cache_k_win[0].transpose(0, 2, 3, 1), cache_v_win[0].transpose(0, 2, 3, 1))
    o_s = o4.transpose(2, 0, 1, 3).reshape(dec_seq * nseq, D_ATTN)
    x1_s = _out_proj(xs, a_s.reshape(dec_seq * nseq, D_CONV), o_s, w_out_b)
    past_f = _chunk_cols(state_conv_ffn[0].transpose(1, 0, 2).reshape(2 * nseq, D_FF))
    y_s, st_f = _ffn_sample(x1_s, past_f, norm_ffn_g, wg3, wu3, cw3, wd3, gf)

    y_sample = y_s.reshape(dec_seq, nseq, D_MODEL).transpose(1, 0, 2)
    new_rows = lambda t: t.reshape(dec_seq, nseq, N_HEADS, HEAD_DIM).transpose(1, 0, 2, 3)[None]
    conv_a_sample = st_a.transpose(1, 0, 2)[None]
    conv_ffn_sample = _unchunk_cols(st_f).reshape(2, nseq, D_FF).transpose(1, 0, 2)[None]

    return (y_p, y_sample, heads(k_p), heads(v_p), conv_a_prompt, conv_ffn_prompt,
            new_rows(k_s), new_rows(v_s), conv_a_sample, conv_ffn_sample)
```

```python
import functools

import jax
import jax.numpy as jnp
from jax import lax
from jax.experimental import pallas as pl
from jax.experimental.pallas import tpu as pltpu

D_MODEL = 1024
D_CONV = 256
D_ATTN = 768
HEAD_DIM = 64
N_HEADS = 12
D_IN = 2 * D_CONV + 3 * D_ATTN
CONV_A_WIDTH = 31
D_FF = 2816
EPS = 1e-6
NEG_INF = -1e30
SCALE = HEAD_DIM ** -0.5

DILATIONS = (1, 4, 16)
NEIGHBOURS = 128
QBLOCK = 128
SPAN = DILATIONS[-1] * QBLOCK
CACHE_LEN = 2048
DEC_SEQ = 8

LANES = 128
FF_CHUNK = 256
N_FF_CHUNKS = D_FF // FF_CHUNK
ROW_TILE = 512
VMEM_LIMIT = 56 * 1024 * 1024

F32 = jnp.float32
BF16 = jnp.bfloat16


def _const_spec(shape):
    nd = len(shape)
    return pl.BlockSpec(shape, lambda *_: (0,) * nd, pipeline_mode=pl.Buffered(1))


def _params(n_axes):
    return pltpu.CompilerParams(
        dimension_semantics=("arbitrary",) * n_axes, vmem_limit_bytes=VMEM_LIMIT)


def _rms_norm(x, g):
    ms = jnp.mean(x * x, axis=-1, keepdims=True)
    return x * lax.rsqrt(ms + EPS) * g


def _dot(a, b):
    return jnp.dot(a, b, preferred_element_type=F32)


def _dot_nt(a, b):
    return lax.dot_general(a, b, (((1,), (1,)), ((), ())), preferred_element_type=F32)


K_COLS = (2 * D_CONV + D_ATTN, 2 * D_CONV + 2 * D_ATTN)
V_COLS = (2 * D_CONV + 2 * D_ATTN, D_IN)


def _normed(x_ref, g_ref):
    return _rms_norm(x_ref[...], g_ref[...]).astype(BF16)


def _project_glu(h, w_ref, glu_ref):
    a_val = _dot(h, w_ref[:, 0:D_CONV])
    a_gate = _dot(h, w_ref[:, D_CONV:2 * D_CONV])
    glu_ref[...] = a_val * jax.nn.sigmoid(a_gate)


def _project_qkv(h, w_ref, q_ref, k_ref, v_ref):
    q_ref[...] = _dot(h, w_ref[:, 2 * D_CONV:K_COLS[0]])
    k_ref[...] = _dot(h, w_ref[:, K_COLS[0]:K_COLS[1]])
    v_ref[...] = _dot(h, w_ref[:, V_COLS[0]:V_COLS[1]])


def _in_proj_sample_kernel(x_ref, g_ref, w_ref, glu_ref, q_ref, k_ref, v_ref):
    h = _normed(x_ref, g_ref)
    _project_glu(h, w_ref, glu_ref)
    _project_qkv(h, w_ref, q_ref, k_ref, v_ref)


def _in_proj_prompt_kernel(x_ref, g_ref, w_ref, wkt_ref, wvt_ref, cw_ref, cb_ref, lg_ref, lb_ref,
                           a_ref, q_ref, k_ref, v_ref, kt_ref, vt_ref, tail_ref, win_ref, sh_ref,
                           *, first_window_tile):
    @pl.when(pl.program_id(1) == 0)
    def _():
        win_ref[0:CONV_HALO, :] = jnp.zeros((CONV_HALO, D_CONV), F32)

    @pl.when(pl.program_id(1) > 0)
    def _():
        win_ref[0:CONV_HALO, :] = win_ref[ROW_TILE:ROW_TILE + CONV_HALO, :]

    h = _normed(x_ref, g_ref)
    _project_glu(h, w_ref, win_ref.at[CONV_HALO:CONV_HALO + ROW_TILE])
    tail_ref[...] = win_ref[ROW_TILE:ROW_TILE + CONV_HALO, :]

    first = CONV_HALO - (CONV_A_WIDTH - 1)
    for s in range(1, 8):
        sh_ref[s - 1] = win_ref[s:s + CONV_SHIFTED_ROWS, :]
    for c in range(ROW_TILE // CONV_ROWS):
        r0 = c * CONV_ROWS
        acc = jnp.broadcast_to(cb_ref[...], (CONV_ROWS, D_CONV))
        for j in range(CONV_A_WIDTH):
            whole, s = divmod(first + j, 8)
            src = win_ref if s == 0 else sh_ref.at[s - 1]
            acc = acc + cw_ref[j:j + 1, :] * src[r0 + 8 * whole:r0 + 8 * whole + CONV_ROWS, :]
        a_ref[r0:r0 + CONV_ROWS, :] = _layer_norm_silu(acc, lg_ref[...], lb_ref[...])

    _project_qkv(h, w_ref, q_ref, k_ref, v_ref)

    @pl.when(pl.program_id(1) >= first_window_tile)
    def _():
        kt_ref[...] = _dot_nt(wkt_ref[...], h)
        vt_ref[...] = _dot_nt(wvt_ref[...], h)


def _in_proj_out_shapes(lead):
    return ([jax.ShapeDtypeStruct(lead + (D_CONV,), F32)]
            + [jax.ShapeDtypeStruct(lead + (D_ATTN,), F32)] * 3)


def _in_proj_sample(x2d, g, w_b):
    rows = x2d.shape[0]
    row = lambda width: pl.BlockSpec((ROW_TILE, width), lambda i: (i, 0))
    return pl.pallas_call(
        _in_proj_sample_kernel,
        grid=(rows // ROW_TILE,),
        in_specs=[row(D_MODEL), _const_spec((1, D_MODEL)), _const_spec((D_MODEL, D_IN))],
        out_specs=[row(D_CONV), row(D_ATTN), row(D_ATTN), row(D_ATTN)],
        out_shape=_in_proj_out_shapes((rows,)),
        compiler_params=_params(1),
        name="in_proj_sample",
    )(x2d, g, w_b)


def _in_proj_prompt(x3d, g, w_b, wkt_b, wvt_b, cw, cb, lg, lb):
    batch, seq, _ = x3d.shape
    first = (seq - CACHE_LEN) // ROW_TILE
    row = lambda width: pl.BlockSpec((None, ROW_TILE, width), lambda b, j: (b, j, 0))
    win = pl.BlockSpec((None, D_ATTN, ROW_TILE), lambda b, j: (b, 0, jnp.maximum(j - first, 0)))
    tail = pl.BlockSpec((None, CONV_HALO, D_CONV), lambda b, j: (b, 0, 0))
    return pl.pallas_call(
        functools.partial(_in_proj_prompt_kernel, first_window_tile=first),
        grid=(batch, seq // ROW_TILE),
        in_specs=[row(D_MODEL), _const_spec((1, D_MODEL)), _const_spec((D_MODEL, D_IN)),
                  _const_spec((D_ATTN, D_MODEL)), _const_spec((D_ATTN, D_MODEL)),
                  _const_spec((CONV_A_WIDTH, D_CONV))] + [_const_spec((1, D_CONV))] * 3,
        out_specs=[row(D_CONV), row(D_ATTN), row(D_ATTN), row(D_ATTN), win, win, tail],
        out_shape=_in_proj_out_shapes((batch, seq))
        + [jax.ShapeDtypeStruct((batch, D_ATTN, CACHE_LEN), F32)] * 2
        + [jax.ShapeDtypeStruct((batch, CONV_HALO, D_CONV), F32)],
        scratch_shapes=[pltpu.VMEM((CONV_HALO + ROW_TILE, D_CONV), F32),
                        pltpu.VMEM((7, CONV_SHIFTED_ROWS, D_CONV), F32)],
        compiler_params=_params(2),
        name="in_proj_prompt",
    )(x3d, g, w_b, wkt_b, wvt_b, cw, cb, lg, lb)


def _layer_norm_silu(y, g, b):
    mu = jnp.mean(y, axis=-1, keepdims=True)
    yc = y - mu
    z = yc * lax.rsqrt(jnp.mean(yc * yc, axis=-1, keepdims=True) + EPS) * g + b
    return z * jax.nn.sigmoid(z)


CONV_HALO = 32
CONV_ROWS = 64
CONV_SHIFTED_ROWS = ROW_TILE + CONV_HALO - 8


def _conv_a_sample_kernel(past_ref, glu_ref, w_ref, b_ref, lg_ref, lb_ref, a_ref, state_ref):
    n_past = CONV_A_WIDTH - 1

    def row(u):
        return past_ref[u] if u < n_past else glu_ref[u - n_past]

    for t in range(DEC_SEQ):
        acc = jnp.broadcast_to(b_ref[...], a_ref.shape[1:])
        for j in range(CONV_A_WIDTH):
            acc = acc + w_ref[j:j + 1, :] * row(t + j)
        a_ref[t] = _layer_norm_silu(acc, lg_ref[...], lb_ref[...])
    for u in range(n_past):
        state_ref[u] = row(u + DEC_SEQ)


def _conv_a_sample(past_t, glu_t, w, b, lg, lb):
    n_past, nseq, _ = past_t.shape
    return pl.pallas_call(
        _conv_a_sample_kernel,
        out_shape=[jax.ShapeDtypeStruct((DEC_SEQ, nseq, D_CONV), F32),
                   jax.ShapeDtypeStruct((n_past, nseq, D_CONV), F32)],
        compiler_params=pltpu.CompilerParams(vmem_limit_bytes=VMEM_LIMIT),
        name="conv_a_sample",
    )(past_t, glu_t, w, b, lg, lb)


def _group_geometry(d):
    nb = SPAN // (d * QBLOCK)
    slot = (nb + 1) * QBLOCK
    return nb, slot


def _attn_prompt_kernel(q_ref, k_ref, v_ref, o_ref, *scratch):
    n_g = len(DILATIONS)
    qg = scratch[0:n_g]
    kg = scratch[n_g:2 * n_g]
    vg = scratch[2 * n_g:3 * n_g]
    acc_n = scratch[3 * n_g:4 * n_g]
    den_n = scratch[4 * n_g:5 * n_g]
    max_n = scratch[5 * n_g:6 * n_g]
    stage_ref, s_ref, p_ref = scratch[6 * n_g:]
    sp = pl.program_id(2)
    lane = lax.broadcasted_iota(jnp.int32, (QBLOCK, LANES), 1)
    head0 = lane < HEAD_DIM

    @pl.when(sp == 0)
    def _():
        for g, d in enumerate(DILATIONS):
            _, slot = _group_geometry(d)
            vg[g][:, LANES:] = jnp.ones((d * slot, LANES), BF16)
            for r in range(d):
                kg[g][r * slot:r * slot + QBLOCK, :] = jnp.zeros((QBLOCK, LANES), BF16)
                vg[g][r * slot:r * slot + QBLOCK, 0:LANES] = jnp.zeros((QBLOCK, LANES), BF16)

    @pl.when(sp > 0)
    def _():
        for g, d in enumerate(DILATIONS):
            nb, slot = _group_geometry(d)
            for r in range(d):
                lo = r * slot
                kg[g][lo:lo + QBLOCK, :] = kg[g][lo + nb * QBLOCK:lo + slot, :]
                vg[g][lo:lo + QBLOCK, :] = vg[g][lo + nb * QBLOCK:lo + slot, :]

    def put(g, r, n, rows_q, x_q, x_k, x_v):
        _, slot = _group_geometry(DILATIONS[g])
        qg[g][rows_q, :] = (x_q * SCALE).astype(BF16)
        lo = r * slot + QBLOCK
        kg[g][lo:lo + n, :] = x_k.astype(BF16)
        vg[g][lo:lo + n, 0:LANES] = x_v.astype(BF16)

    put(0, 0, SPAN, slice(0, SPAN), q_ref[...], k_ref[...], v_ref[...])
    d1, d2 = DILATIONS[1], DILATIONS[2] // DILATIONS[1]
    n1 = SPAN // d1
    for r in range(d1):
        xs = [ref[pl.ds(r, n1, stride=d1), :] for ref in (q_ref, k_ref, v_ref)]
        for t, x in enumerate(xs):
            stage_ref[t, r * n1:(r + 1) * n1, :] = x
        put(1, r, n1, slice(r * n1, (r + 1) * n1), *xs)
    n2 = n1 // d2
    for r in range(DILATIONS[2]):
        rows = pl.ds((r % d1) * n1 + r // d1, n2, stride=d2)
        xs = [stage_ref[t, rows, :] for t in range(3)]
        put(2, r, n2, slice(r * n2, (r + 1) * n2), *xs)

    qi = lax.broadcasted_iota(jnp.int32, (QBLOCK, 2 * QBLOCK), 0)
    kc = lax.broadcasted_iota(jnp.int32, (QBLOCK, 2 * QBLOCK), 1)
    band = (kc >= qi) & (kc <= qi + NEIGHBOURS)

    def split(u, nb):
        if nb == 1:
            return u, 0
        return lax.shift_right_logical(u, nb.bit_length() - 1), lax.bitwise_and(u, nb - 1)

    def key_rows(u, nb):
        r, _ = split(u, nb)
        return pl.ds(pl.multiple_of((u + r) * QBLOCK, QBLOCK), 2 * QBLOCK)

    def scores(u):
        q0 = pl.multiple_of(u * QBLOCK, QBLOCK)
        slot_w = lax.bitwise_and(u, 1)
        zero = jnp.zeros((QBLOCK, LANES), BF16)
        for g, d in enumerate(DILATIONS):
            nb, _ = _group_geometry(d)
            kb = kg[g][key_rows(u, nb), :]
            qb = qg[g][pl.ds(q0, QBLOCK), :]
            s_ref[slot_w, 2 * g] = _dot_nt(jnp.where(head0, qb, zero), kb)
            s_ref[slot_w, 2 * g + 1] = _dot_nt(jnp.where(head0, zero, qb), kb)

    def result_rows(u, g):
        if g == 2:
            return pl.ds(lax.bitwise_and(u, d1 - 1) * n1 + lax.shift_right_logical(u, 2), QBLOCK,
                         stride=d2)
        return pl.ds(pl.multiple_of(u * QBLOCK, QBLOCK), QBLOCK)

    def softmax(u):
        slot = lax.bitwise_and(u, 1)
        for g, d in enumerate(DILATIONS):
            nb, _ = _group_geometry(d)
            _, blk = split(u, nb)
            first_key = jnp.where(sp * nb + blk == 0, QBLOCK, 0)
            mask = band & (kc >= first_key)
            ms = []
            for h in range(2):
                s = jnp.where(mask, s_ref[slot, 2 * g + h], NEG_INF)
                m = jnp.max(s, axis=-1, keepdims=True)
                p_ref[slot, 2 * g + h] = jnp.exp(s - m).astype(BF16)
                ms.append(m)
            max_n[g][result_rows(u, g), :] = jnp.where(head0, ms[0], ms[1])

    def values(u):
        slot_r = lax.bitwise_and(u, 1)
        for g, d in enumerate(DILATIONS):
            nb, _ = _group_geometry(d)
            vb = vg[g][key_rows(u, nb), :]
            e0 = _dot(p_ref[slot_r, 2 * g], vb)
            e1 = _dot(p_ref[slot_r, 2 * g + 1], vb)
            rows = result_rows(u, g)
            acc_n[g][rows, :] = jnp.where(head0, e0[:, :LANES], e1[:, :LANES])
            den_n[g][rows, :] = jnp.where(head0, e0[:, LANES:], e1[:, LANES:])

    def unit(u, carry):
        values(u - 2)
        softmax(u - 1)
        scores(u)
        return carry

    last = jnp.int32(SPAN // QBLOCK - 1)
    scores(jnp.int32(0))
    softmax(jnp.int32(0))
    scores(jnp.int32(1))
    lax.fori_loop(2, SPAN // QBLOCK, unit, 0, unroll=7)
    values(last - 1)
    softmax(last)
    values(last)

    def merge(c, carry):
        major = pl.ds(pl.multiple_of(c * QBLOCK, QBLOCK), QBLOCK)
        by_pos = pl.ds(lax.bitwise_and(c, d1 - 1) * (QBLOCK * d1) + lax.shift_right_logical(c, 2),
                       QBLOCK, stride=d1)
        rows = (by_pos, major, major)
        ms = [max_n[g][rows[g], :] for g in range(n_g)]
        m_all = functools.reduce(jnp.maximum, ms)
        ws = [jnp.exp(m - m_all) for m in ms]
        num = sum(w * acc_n[g][rows[g], :] for g, w in enumerate(ws))
        den = sum(w * den_n[g][rows[g], :] for g, w in enumerate(ws))
        o_ref[by_pos, :] = num / den
        return carry

    lax.fori_loop(0, SPAN // QBLOCK, merge, 0)


def _attn_prompt(q, k, v):
    batch, seq, _ = q.shape
    blk = pl.BlockSpec((None, SPAN, LANES), lambda b, hp, s: (b, s, hp))
    n_g = len(DILATIONS)
    scratch = [pltpu.VMEM((SPAN, LANES), BF16) for _ in DILATIONS]
    scratch += [pltpu.VMEM((d * _group_geometry(d)[1], LANES), BF16) for d in DILATIONS]
    scratch += [pltpu.VMEM((d * _group_geometry(d)[1], 2 * LANES), BF16) for d in DILATIONS]
    scratch += [pltpu.VMEM((SPAN, LANES), F32) for _ in range(3 * n_g)]
    scratch += [pltpu.VMEM((3, SPAN, LANES), F32),
                pltpu.VMEM((2, 2 * n_g, QBLOCK, 2 * QBLOCK), F32),
                pltpu.VMEM((2, 2 * n_g, QBLOCK, 2 * QBLOCK), BF16)]
    return pl.pallas_call(
        _attn_prompt_kernel,
        grid=(batch, D_ATTN // LANES, seq // SPAN),
        in_specs=[blk, blk, blk],
        out_specs=blk,
        out_shape=jax.ShapeDtypeStruct((batch, seq, D_ATTN), F32),
        scratch_shapes=scratch,
        compiler_params=_params(3),
        name="attn_prompt",
    )(q, k, v)


HEADS_PER_STEP = 6


def _group_count(dist):
    total = None
    for d in DILATIONS:
        sh = d.bit_length() - 1
        hit = ((dist & (d - 1)) == 0) & ((dist >> sh) <= NEIGHBOURS) & (dist >= 0)
        total = hit.astype(F32) if total is None else total + hit.astype(F32)
    return total


def _attn_sample_kernel(q_ref, kn_ref, vn_ref, kc_ref, vc_ref, o_ref):
    qi = lax.broadcasted_iota(jnp.int32, (DEC_SEQ, CACHE_LEN), 0)
    pos = lax.broadcasted_iota(jnp.int32, (DEC_SEQ, CACHE_LEN), 1)
    cnt_c = _group_count(CACHE_LEN + qi - pos)
    qn = lax.broadcasted_iota(jnp.int32, (DEC_SEQ, DEC_SEQ), 0)
    kn = lax.broadcasted_iota(jnp.int32, (DEC_SEQ, DEC_SEQ), 1)
    cnt_n = _group_count(qn - kn)
    for h in range(HEADS_PER_STEP):
        qh = (q_ref[h] * SCALE).astype(BF16)
        s_c = jnp.where(cnt_c > 0, _dot(qh, kc_ref[h].astype(BF16)), NEG_INF)
        s_n = jnp.where(cnt_n > 0, _dot_nt(qh, kn_ref[h].astype(BF16)), NEG_INF)
        m = jnp.maximum(jnp.max(s_c, axis=-1, keepdims=True), jnp.max(s_n, axis=-1, keepdims=True))
        w_c = cnt_c * jnp.exp(s_c - m)
        w_n = cnt_n * jnp.exp(s_n - m)
        den = jnp.sum(w_c, axis=-1, keepdims=True) + jnp.sum(w_n, axis=-1, keepdims=True)
        num = _dot_nt(w_c.astype(BF16), vc_ref[h].astype(BF16)) + _dot(w_n.astype(BF16), vn_ref[h].astype(BF16))
        o_ref[h] = num / den


def _attn_sample(q4, kn4, vn4, kc_t, vc_t):
    nseq = q4.shape[0]
    new = pl.BlockSpec((None, HEADS_PER_STEP, DEC_SEQ, HEAD_DIM), lambda s, h: (s, h, 0, 0))
    cache = pl.BlockSpec((None, HEADS_PER_STEP, HEAD_DIM, CACHE_LEN), lambda s, h: (s, h, 0, 0))
    return pl.pallas_call(
        _attn_sample_kernel,
        grid=(nseq, N_HEADS // HEADS_PER_STEP),
        in_specs=[new, new, new, cache, cache],
        out_specs=new,
        out_shape=jax.ShapeDtypeStruct((nseq, N_HEADS, DEC_SEQ, HEAD_DIM), F32),
        compiler_params=_params(2),
        name="attn_sample",
    )(q4, kn4, vn4, kc_t, vc_t)


FFN_HALO = 8


def _mix_ffn(x_ref, a_ref, o_ref, wo_ref, g2_ref, wg_ref, wu_ref, cw_ref, wd_ref, gf_ref, y_ref,
             x1_ref, h_ref, act_ref, conv):
    x1 = x_ref[...] + _dot(a_ref[...].astype(BF16), wo_ref[0:D_CONV, :])
    x1 = x1 + _dot(o_ref[...].astype(BF16), wo_ref[D_CONV:, :])
    x1_ref[...] = x1
    h_ref[...] = _rms_norm(x1, g2_ref[...]).astype(BF16)
    for c in range(N_FF_CHUNKS):
        cols = slice(c * FF_CHUNK, (c + 1) * FF_CHUNK)
        h = h_ref[...]
        g = conv(c, cols, _dot(h, wg_ref[:, cols]), cw_ref[:, cols])
        act_ref[:, cols] = ((g * jax.nn.sigmoid(g)) * _dot(h, wu_ref[:, cols])).astype(BF16)
    y_ref[...] = _rms_norm(x1_ref[...] + _dot(act_ref[...], wd_ref[...]), gf_ref[...])


def _conv_taps(cw, now, back1, back2):
    return cw[2:3, :] * now + cw[1:2, :] * back1 + cw[0:1, :] * back2 + cw[3:4, :]


def _ffn_prompt_kernel(x_ref, a_ref, o_ref, wo_ref, g2_ref, wg_ref, wu_ref, cw_ref, wd_ref, gf_ref,
                       y_ref, st_ref, x1_ref, h_ref, act_ref, gbuf_ref, carry_ref):
    @pl.when(pl.program_id(1) == 0)
    def _():
        carry_ref[...] = jnp.zeros(carry_ref.shape, F32)

    def conv(c, cols, gp, cw):
        buf = gbuf_ref.at[c % 2]
        buf[0:FFN_HALO, :] = carry_ref[:, cols]
        buf[FFN_HALO:, :] = gp
        tail = gp[ROW_TILE - FFN_HALO:, :]
        carry_ref[:, cols] = tail
        st_ref[:, cols] = tail
        return _conv_taps(cw, gp, buf[FFN_HALO - 1:FFN_HALO - 1 + ROW_TILE, :],
                          buf[FFN_HALO - 2:FFN_HALO - 2 + ROW_TILE, :])

    _mix_ffn(x_ref, a_ref, o_ref, wo_ref, g2_ref, wg_ref, wu_ref, cw_ref, wd_ref, gf_ref, y_ref,
             x1_ref, h_ref, act_ref, conv)


def _ffn_weight_specs():
    return [_const_spec((D_MODEL, D_MODEL)), _const_spec((1, D_MODEL)),
            _const_spec((D_MODEL, D_FF)), _const_spec((D_MODEL, D_FF)),
            _const_spec((8, D_FF)), _const_spec((D_FF, D_MODEL)), _const_spec((1, D_MODEL))]


def _ffn_scratch(rows):
    return [pltpu.VMEM((rows, D_MODEL), F32), pltpu.VMEM((rows, D_MODEL), BF16),
            pltpu.VMEM((rows, D_FF), BF16)]


def _ffn_prompt(x, a, o, wo, g2, wg, wu, cw, wd, gf):
    batch, seq, _ = x.shape
    row = lambda width: pl.BlockSpec((None, ROW_TILE, width), lambda b, j: (b, j, 0))
    st = pl.BlockSpec((None, FFN_HALO, D_FF), lambda b, j: (b, 0, 0))
    return pl.pallas_call(
        _ffn_prompt_kernel,
        grid=(batch, seq // ROW_TILE),
        in_specs=[row(D_MODEL), row(D_CONV), row(D_ATTN)] + _ffn_weight_specs(),
        out_specs=[row(D_MODEL), st],
        out_shape=[jax.ShapeDtypeStruct((batch, seq, D_MODEL), F32),
                   jax.ShapeDtypeStruct((batch, FFN_HALO, D_FF), F32)],
        scratch_shapes=_ffn_scratch(ROW_TILE)
        + [pltpu.VMEM((2, FFN_HALO + ROW_TILE, FF_CHUNK), F32), pltpu.VMEM((FFN_HALO, D_FF), F32)],
        compiler_params=_params(2),
        name="ffn_prompt",
    )(x, a, o, wo, g2, wg, wu, cw, wd, gf)


def _ffn_sample_kernel(x_ref, a_ref, o_ref, past_ref, wo_ref, g2_ref, wg_ref, wu_ref, cw_ref, wd_ref,
                       gf_ref, y_ref, st_ref, x1_ref, h_ref, act_ref, gbuf_ref):
    rows = x_ref.shape[0]
    nseq = rows // DEC_SEQ

    def conv(c, cols, gp, cw):
        buf = gbuf_ref.at[c % 2]
        buf[0:2 * nseq, :] = past_ref[:, cols]
        buf[2 * nseq:, :] = gp
        st_ref[:, cols] = gp[rows - 2 * nseq:, :]
        return _conv_taps(cw, gp, buf[nseq:nseq + rows, :], buf[0:rows, :])

    _mix_ffn(x_ref, a_ref, o_ref, wo_ref, g2_ref, wg_ref, wu_ref, cw_ref, wd_ref, gf_ref, y_ref,
             x1_ref, h_ref, act_ref, conv)


def _ffn_sample(x, a, o, past, wo, g2, wg, wu, cw, wd, gf):
    rows = x.shape[0]
    nseq = rows // DEC_SEQ
    return pl.pallas_call(
        _ffn_sample_kernel,
        out_shape=[jax.ShapeDtypeStruct((rows, D_MODEL), F32),
                   jax.ShapeDtypeStruct((2 * nseq, D_FF), F32)],
        scratch_shapes=_ffn_scratch(rows) + [pltpu.VMEM((2, 2 * nseq + rows, FF_CHUNK), F32)],
        compiler_params=pltpu.CompilerParams(vmem_limit_bytes=VMEM_LIMIT),
        name="ffn_sample",
    )(x, a, o, past, wo, g2, wg, wu, cw, wd, gf)


def kernel(x_prompt, x_sample, cache_k_win, cache_v_win, state_conv_a, state_conv_ffn, norm_mix_g, w_in, conv_a_w, conv_a_b, ln_a_g, ln_a_b, w_out, norm_ffn_g, w_ffn_gate, w_ffn_up, ffn_conv_w, ffn_conv_b, w_ffn_down, norm_final_g):
    batch, seq, _ = x_prompt.shape
    nseq, dec_seq, _ = x_sample.shape
    assert w_in.shape[0] == 1, "single trunk layer"
    assert dec_seq == DEC_SEQ and cache_k_win.shape[2] == CACHE_LEN
    assert seq % SPAN == 0 and seq >= CACHE_LEN and (nseq * dec_seq) % ROW_TILE == 0

    g1 = norm_mix_g
    w_in_b = w_in[0].astype(BF16)
    wkt_b = w_in_b[:, K_COLS[0]:K_COLS[1]].T
    wvt_b = w_in_b[:, V_COLS[0]:V_COLS[1]].T
    ca = (conv_a_w[0], conv_a_b, ln_a_g, ln_a_b)
    cw = jnp.concatenate([ffn_conv_w[0], ffn_conv_b, jnp.zeros((4, D_FF), F32)], axis=0)
    ffn_w = (w_out[0].astype(BF16), norm_ffn_g, w_ffn_gate[0].astype(BF16), w_ffn_up[0].astype(BF16),
             cw, w_ffn_down[0].astype(BF16), norm_final_g[None, :])

    a_p, q_p, k_p, v_p, kt_win, vt_win, glu_tail = _in_proj_prompt(
        x_prompt, g1, w_in_b, wkt_b, wvt_b, *ca)
    o_p = _attn_prompt(q_p, k_p, v_p)
    y_p, st_p = _ffn_prompt(x_prompt, a_p, o_p, *ffn_w)

    window = lambda t: t.reshape(batch, N_HEADS, HEAD_DIM, CACHE_LEN).transpose(0, 3, 1, 2)[None]
    conv_a_prompt = glu_tail[:, CONV_HALO - (CONV_A_WIDTH - 1):][None]
    conv_ffn_prompt = st_p[:, FFN_HALO - 2:][None]

    rows = dec_seq * nseq
    xs = x_sample.transpose(1, 0, 2).reshape(rows, D_MODEL)
    glu_s, q_s, k_s, v_s = _in_proj_sample(xs, g1, w_in_b)
    a_s, st_a = _conv_a_sample(state_conv_a[0].transpose(1, 0, 2),
                               glu_s.reshape(dec_seq, nseq, D_CONV), *ca)
    per_head = lambda t: t.reshape(dec_seq, nseq, N_HEADS, HEAD_DIM).transpose(1, 2, 0, 3)
    o4 = _attn_sample(per_head(q_s), per_head(k_s), per_head(v_s),
                      cache_k_win[0].transpose(0, 2, 3, 1), cache_v_win[0].transpose(0, 2, 3, 1))
    o_s = o4.transpose(2, 0, 1, 3).reshape(rows, D_ATTN)
    past_f = state_conv_ffn[0].transpose(1, 0, 2).reshape(2 * nseq, D_FF)
    y_s, st_f = _ffn_sample(xs, a_s.reshape(rows, D_CONV), o_s, past_f, *ffn_w)

    y_sample = y_s.reshape(dec_seq, nseq, D_MODEL).transpose(1, 0, 2)
    new_rows = lambda t: t.reshape(dec_seq, nseq, N_HEADS, HEAD_DIM).transpose(1, 0, 2, 3)[None]
    conv_a_sample = st_a.transpose(1, 0, 2)[None]
    conv_ffn_sample = st_f.reshape(2, nseq, D_FF).transpose(1, 0, 2)[None]

    return (y_p, y_sample, window(kt_win), window(vt_win), conv_a_prompt, conv_ffn_prompt,
            new_rows(k_s), new_rows(v_s), conv_a_sample, conv_ffn_sample)
```

```python
import functools

import jax
import jax.numpy as jnp
from jax import lax
from jax.experimental import pallas as pl
from jax.experimental.pallas import tpu as pltpu

D_MODEL = 1024
D_CONV = 256
D_ATTN = 768
HEAD_DIM = 64
N_HEADS = 12
D_IN = 2 * D_CONV + 3 * D_ATTN
CONV_A_WIDTH = 31
D_FF = 2816
EPS = 1e-6
NEG_INF = -1e30
SCALE = HEAD_DIM ** -0.5

DILATIONS = (1, 4, 16)
NEIGHBOURS = 128
QBLOCK = 128
SPAN = DILATIONS[-1] * QBLOCK
CACHE_LEN = 2048
DEC_SEQ = 8

LANES = 128
FF_CHUNK = 256
N_FF_CHUNKS = D_FF // FF_CHUNK
ROW_TILE = 256
VMEM_LIMIT = 56 * 1024 * 1024

F32 = jnp.float32
BF16 = jnp.bfloat16


def _const_spec(shape):
    nd = len(shape)
    return pl.BlockSpec(shape, lambda *_: (0,) * nd, pipeline_mode=pl.Buffered(1))


def _params(n_axes):
    return pltpu.CompilerParams(
        dimension_semantics=("arbitrary",) * n_axes, vmem_limit_bytes=VMEM_LIMIT)


def _rms_norm(x, g):
    ms = jnp.mean(x * x, axis=-1, keepdims=True)
    return x * lax.rsqrt(ms + EPS) * g


def _dot(a, b):
    return jnp.dot(a, b, preferred_element_type=F32)


def _dot_nt(a, b):
    return lax.dot_general(a, b, (((1,), (1,)), ((), ())), preferred_element_type=F32)


K_COLS = (2 * D_CONV + D_ATTN, 2 * D_CONV + 2 * D_ATTN)
V_COLS = (2 * D_CONV + 2 * D_ATTN, D_IN)


def _normed(x_ref, g_ref):
    return _rms_norm(x_ref[...], g_ref[...]).astype(BF16)


def _project_glu(h, w_ref, glu_ref):
    a_val = _dot(h, w_ref[:, 0:D_CONV])
    a_gate = _dot(h, w_ref[:, D_CONV:2 * D_CONV])
    glu_ref[...] = a_val * jax.nn.sigmoid(a_gate)


def _project_qkv(h, w_ref, q_ref, k_ref, v_ref):
    q_ref[...] = _dot(h, w_ref[:, 2 * D_CONV:K_COLS[0]])
    k_ref[...] = _dot(h, w_ref[:, K_COLS[0]:K_COLS[1]])
    v_ref[...] = _dot(h, w_ref[:, V_COLS[0]:V_COLS[1]])


def _in_proj_sample_kernel(x_ref, g_ref, w_ref, glu_ref, q_ref, k_ref, v_ref):
    h = _normed(x_ref, g_ref)
    _project_glu(h, w_ref, glu_ref)
    _project_qkv(h, w_ref, q_ref, k_ref, v_ref)


def _in_proj_prompt_kernel(x_ref, g_ref, w_ref, wkt_ref, wvt_ref, cw_ref, cb_ref, lg_ref, lb_ref,
                           sq_ref, skn_ref, skc_ref,
                           a_ref, q_ref, k_ref, v_ref, kt_ref, vt_ref, tail_ref, spc_ref, spn_ref,
                           win_ref, sh_ref, *, first_window_tile):
    @pl.when(pl.program_id(1) == 0)
    def _():
        win_ref[0:CONV_HALO, :] = jnp.zeros((CONV_HALO, D_CONV), F32)

    @pl.when(pl.program_id(1) > 0)
    def _():
        win_ref[0:CONV_HALO, :] = win_ref[ROW_TILE:ROW_TILE + CONV_HALO, :]

    h = _normed(x_ref, g_ref)
    _project_glu(h, w_ref, win_ref.at[CONV_HALO:CONV_HALO + ROW_TILE])
    tail_ref[...] = win_ref[ROW_TILE:ROW_TILE + CONV_HALO, :]

    first = CONV_HALO - (CONV_A_WIDTH - 1)
    for s in range(1, 8):
        sh_ref[s - 1] = win_ref[s:s + CONV_SHIFTED_ROWS, :]
    for c in range(ROW_TILE // CONV_ROWS):
        r0 = c * CONV_ROWS
        acc = jnp.broadcast_to(cb_ref[...], (CONV_ROWS, D_CONV))
        for j in range(CONV_A_WIDTH):
            whole, s = divmod(first + j, 8)
            src = win_ref if s == 0 else sh_ref.at[s - 1]
            acc = acc + cw_ref[j:j + 1, :] * src[r0 + 8 * whole:r0 + 8 * whole + CONV_ROWS, :]
        a_ref[r0:r0 + CONV_ROWS, :] = _layer_norm_silu(acc, lg_ref[...], lb_ref[...])

    _project_qkv(h, w_ref, q_ref, k_ref, v_ref)
    _sample_probs(sq_ref, skn_ref, skc_ref, spc_ref, spn_ref)

    @pl.when(pl.program_id(1) >= first_window_tile)
    def _():
        kt_ref[...] = _dot_nt(wkt_ref[...], h)
        vt_ref[...] = _dot_nt(wvt_ref[...], h)


def _in_proj_out_shapes(lead):
    return ([jax.ShapeDtypeStruct(lead + (D_CONV,), F32)]
            + [jax.ShapeDtypeStruct(lead + (D_ATTN,), F32)] * 3)


def _in_proj_sample(x2d, g, w_b):
    rows = x2d.shape[0]
    row = lambda width: pl.BlockSpec((ROW_TILE, width), lambda i: (i, 0))
    return pl.pallas_call(
        _in_proj_sample_kernel,
        grid=(rows // ROW_TILE,),
        in_specs=[row(D_MODEL), _const_spec((1, D_MODEL)), _const_spec((D_MODEL, D_IN))],
        out_specs=[row(D_CONV), row(D_ATTN), row(D_ATTN), row(D_ATTN)],
        out_shape=_in_proj_out_shapes((rows,)),
        compiler_params=_params(1),
        name="in_proj_sample",
    )(x2d, g, w_b)


def _in_proj_prompt(x3d, g, w_b, wkt_b, wvt_b, cw, cb, lg, lb, q4, kn4, kc_t):
    batch, seq, _ = x3d.shape
    nseq = q4.shape[0]
    steps = seq // ROW_TILE
    first = (seq - CACHE_LEN) // ROW_TILE
    row = lambda width: pl.BlockSpec((None, ROW_TILE, width), lambda b, j: (b, j, 0))
    win = pl.BlockSpec((None, D_ATTN, ROW_TILE), lambda b, j: (b, 0, jnp.maximum(j - first, 0)))
    tail = pl.BlockSpec((None, CONV_HALO, D_CONV), lambda b, j: (b, 0, 0))
    new = _per_sequence_spec((N_HEADS, DEC_SEQ, HEAD_DIM), steps)
    return pl.pallas_call(
        functools.partial(_in_proj_prompt_kernel, first_window_tile=first),
        grid=(batch, steps),
        in_specs=[row(D_MODEL), _const_spec((1, D_MODEL)), _const_spec((D_MODEL, D_IN)),
                  _const_spec((D_ATTN, D_MODEL)), _const_spec((D_ATTN, D_MODEL)),
                  _const_spec((CONV_A_WIDTH, D_CONV))] + [_const_spec((1, D_CONV))] * 3
        + [new, new, _per_sequence_spec((N_HEADS, HEAD_DIM, CACHE_LEN), steps)],
        out_specs=[row(D_CONV), row(D_ATTN), row(D_ATTN), row(D_ATTN), win, win, tail,
                   _per_sequence_spec((N_HEADS, DEC_SEQ, CACHE_LEN), steps),
                   _per_sequence_spec((N_HEADS, DEC_SEQ, DEC_SEQ), steps)],
        out_shape=_in_proj_out_shapes((batch, seq))
        + [jax.ShapeDtypeStruct((batch, D_ATTN, CACHE_LEN), F32)] * 2
        + [jax.ShapeDtypeStruct((batch, CONV_HALO, D_CONV), F32),
           jax.ShapeDtypeStruct((nseq, N_HEADS, DEC_SEQ, CACHE_LEN), BF16),
           jax.ShapeDtypeStruct((nseq, N_HEADS, DEC_SEQ, DEC_SEQ), F32)],
        scratch_shapes=[pltpu.VMEM((CONV_HALO + ROW_TILE, D_CONV), F32),
                        pltpu.VMEM((7, CONV_SHIFTED_ROWS, D_CONV), F32)],
        compiler_params=_params(2),
        name="in_proj_prompt",
    )(x3d, g, w_b, wkt_b, wvt_b, cw, cb, lg, lb, q4, kn4, kc_t)


def _layer_norm_silu(y, g, b):
    mu = jnp.mean(y, axis=-1, keepdims=True)
    yc = y - mu
    z = yc * lax.rsqrt(jnp.mean(yc * yc, axis=-1, keepdims=True) + EPS) * g + b
    return z * jax.nn.sigmoid(z)


CONV_HALO = 32
CONV_ROWS = 64
CONV_SHIFTED_ROWS = ROW_TILE + CONV_HALO - 8


def _conv_a_sample_kernel(past_ref, glu_ref, w_ref, b_ref, lg_ref, lb_ref, a_ref, state_ref):
    n_past = CONV_A_WIDTH - 1

    def row(u):
        return past_ref[u] if u < n_past else glu_ref[u - n_past]

    for t in range(DEC_SEQ):
        acc = jnp.broadcast_to(b_ref[...], a_ref.shape[1:])
        for j in range(CONV_A_WIDTH):
            acc = acc + w_ref[j:j + 1, :] * row(t + j)
        a_ref[t] = _layer_norm_silu(acc, lg_ref[...], lb_ref[...])
    for u in range(n_past):
        state_ref[u] = row(u + DEC_SEQ)


def _conv_a_sample(past_t, glu_t, w, b, lg, lb):
    n_past, nseq, _ = past_t.shape
    return pl.pallas_call(
        _conv_a_sample_kernel,
        out_shape=[jax.ShapeDtypeStruct((DEC_SEQ, nseq, D_CONV), F32),
                   jax.ShapeDtypeStruct((n_past, nseq, D_CONV), F32)],
        compiler_params=pltpu.CompilerParams(vmem_limit_bytes=VMEM_LIMIT),
        name="conv_a_sample",
    )(past_t, glu_t, w, b, lg, lb)


def _group_geometry(d):
    nb = SPAN // (d * QBLOCK)
    slot = (nb + 1) * QBLOCK
    return nb, slot


def _attn_prompt_kernel(q_ref, k_ref, v_ref, o_ref, *scratch):
    n_g = len(DILATIONS)
    qg = scratch[0:n_g]
    kg = scratch[n_g:2 * n_g]
    vg = scratch[2 * n_g:3 * n_g]
    acc_n = scratch[3 * n_g:4 * n_g]
    den_n = scratch[4 * n_g:5 * n_g]
    max_n = scratch[5 * n_g:6 * n_g]
    stage_ref, s_ref, p_ref = scratch[6 * n_g:]
    sp = pl.program_id(2)
    lane = lax.broadcasted_iota(jnp.int32, (QBLOCK, LANES), 1)
    head0 = lane < HEAD_DIM

    @pl.when(sp == 0)
    def _():
        for g, d in enumerate(DILATIONS):
            _, slot = _group_geometry(d)
            vg[g][:, LANES:] = jnp.ones((d * slot, LANES), BF16)
            for r in range(d):
                kg[g][r * slot:r * slot + QBLOCK, :] = jnp.zeros((QBLOCK, LANES), BF16)
                vg[g][r * slot:r * slot + QBLOCK, 0:LANES] = jnp.zeros((QBLOCK, LANES), BF16)

    @pl.when(sp > 0)
    def _():
        for g, d in enumerate(DILATIONS):
            nb, slot = _group_geometry(d)
            for r in range(d):
                lo = r * slot
                kg[g][lo:lo + QBLOCK, :] = kg[g][lo + nb * QBLOCK:lo + slot, :]
                vg[g][lo:lo + QBLOCK, :] = vg[g][lo + nb * QBLOCK:lo + slot, :]

    def put(g, r, n, rows_q, x_q, x_k, x_v):
        _, slot = _group_geometry(DILATIONS[g])
        qg[g][rows_q, :] = (x_q * SCALE).astype(BF16)
        lo = r * slot + QBLOCK
        kg[g][lo:lo + n, :] = x_k.astype(BF16)
        vg[g][lo:lo + n, 0:LANES] = x_v.astype(BF16)

    put(0, 0, SPAN, slice(0, SPAN), q_ref[...], k_ref[...], v_ref[...])
    d1, d2 = DILATIONS[1], DILATIONS[2] // DILATIONS[1]
    n1 = SPAN // d1
    for r in range(d1):
        xs = [ref[pl.ds(r, n1, stride=d1), :] for ref in (q_ref, k_ref, v_ref)]
        for t, x in enumerate(xs):
            stage_ref[t, r * n1:(r + 1) * n1, :] = x
        put(1, r, n1, slice(r * n1, (r + 1) * n1), *xs)
    n2 = n1 // d2
    for r in range(DILATIONS[2]):
        rows = pl.ds((r % d1) * n1 + r // d1, n2, stride=d2)
        xs = [stage_ref[t, rows, :] for t in range(3)]
        put(2, r, n2, slice(r * n2, (r + 1) * n2), *xs)

    qi = lax.broadcasted_iota(jnp.int32, (QBLOCK, 2 * QBLOCK), 0)
    kc = lax.broadcasted_iota(jnp.int32, (QBLOCK, 2 * QBLOCK), 1)
    band = (kc >= qi) & (kc <= qi + NEIGHBOURS)

    def split(u, nb):
        if nb == 1:
            return u, 0
        return lax.shift_right_logical(u, nb.bit_length() - 1), lax.bitwise_and(u, nb - 1)

    def key_rows(u, nb):
        r, _ = split(u, nb)
        return pl.ds(pl.multiple_of((u + r) * QBLOCK, QBLOCK), 2 * QBLOCK)

    def scores(u):
        q0 = pl.multiple_of(u * QBLOCK, QBLOCK)
        slot_w = lax.bitwise_and(u, 1)
        zero = jnp.zeros((QBLOCK, LANES), BF16)
        for g, d in enumerate(DILATIONS):
            nb, _ = _group_geometry(d)
            kb = kg[g][key_rows(u, nb), :]
            qb = qg[g][pl.ds(q0, QBLOCK), :]
            s_ref[slot_w, 2 * g] = _dot_nt(jnp.where(head0, qb, zero), kb)
            s_ref[slot_w, 2 * g + 1] = _dot_nt(jnp.where(head0, zero, qb), kb)

    def result_rows(u, g):
        if g == 2:
            return pl.ds(lax.bitwise_and(u, d1 - 1) * n1 + lax.shift_right_logical(u, 2), QBLOCK,
                         stride=d2)
        return pl.ds(pl.multiple_of(u * QBLOCK, QBLOCK), QBLOCK)

    def softmax(u):
        slot = lax.bitwise_and(u, 1)
        for g, d in enumerate(DILATIONS):
            nb, _ = _group_geometry(d)
            _, blk = split(u, nb)
            first_key = jnp.where(sp * nb + blk == 0, QBLOCK, 0)
            mask = band & (kc >= first_key)
            ms = []
            for h in range(2):
                s = jnp.where(mask, s_ref[slot, 2 * g + h], NEG_INF)
                m = jnp.max(s, axis=-1, keepdims=True)
                p_ref[slot, 2 * g + h] = jnp.exp(s - m).astype(BF16)
                ms.append(m)
            max_n[g][result_rows(u, g), :] = jnp.where(head0, ms[0], ms[1])

    def values(u):
        slot_r = lax.bitwise_and(u, 1)
        for g, d in enumerate(DILATIONS):
            nb, _ = _group_geometry(d)
            vb = vg[g][key_rows(u, nb), :]
            e0 = _dot(p_ref[slot_r, 2 * g], vb)
            e1 = _dot(p_ref[slot_r, 2 * g + 1], vb)
            rows = result_rows(u, g)
            acc_n[g][rows, :] = jnp.where(head0, e0[:, :LANES], e1[:, :LANES])
            den_n[g][rows, :] = jnp.where(head0, e0[:, LANES:], e1[:, LANES:])

    def unit(u, carry):
        values(u - 2)
        softmax(u - 1)
        scores(u)
        return carry

    last = jnp.int32(SPAN // QBLOCK - 1)
    scores(jnp.int32(0))
    softmax(jnp.int32(0))
    scores(jnp.int32(1))
    lax.fori_loop(2, SPAN // QBLOCK, unit, 0, unroll=7)
    values(last - 1)
    softmax(last)
    values(last)

    def merge(c, carry):
        major = pl.ds(pl.multiple_of(c * QBLOCK, QBLOCK), QBLOCK)
        by_pos = pl.ds(lax.bitwise_and(c, d1 - 1) * (QBLOCK * d1) + lax.shift_right_logical(c, 2),
                       QBLOCK, stride=d1)
        rows = (by_pos, major, major)
        ms = [max_n[g][rows[g], :] for g in range(n_g)]
        m_all = functools.reduce(jnp.maximum, ms)
        ws = [jnp.exp(m - m_all) for m in ms]
        num = sum(w * acc_n[g][rows[g], :] for g, w in enumerate(ws))
        den = sum(w * den_n[g][rows[g], :] for g, w in enumerate(ws))
        o_ref[by_pos, :] = num / den
        return carry

    lax.fori_loop(0, SPAN // QBLOCK, merge, 0)


def _attn_prompt(q, k, v):
    batch, seq, _ = q.shape
    blk = pl.BlockSpec((None, SPAN, LANES), lambda b, hp, s: (b, s, hp))
    n_g = len(DILATIONS)
    scratch = [pltpu.VMEM((SPAN, LANES), BF16) for _ in DILATIONS]
    scratch += [pltpu.VMEM((d * _group_geometry(d)[1], LANES), BF16) for d in DILATIONS]
    scratch += [pltpu.VMEM((d * _group_geometry(d)[1], 2 * LANES), BF16) for d in DILATIONS]
    scratch += [pltpu.VMEM((SPAN, LANES), F32) for _ in range(3 * n_g)]
    scratch += [pltpu.VMEM((3, SPAN, LANES), F32),
                pltpu.VMEM((2, 2 * n_g, QBLOCK, 2 * QBLOCK), F32),
                pltpu.VMEM((2, 2 * n_g, QBLOCK, 2 * QBLOCK), BF16)]
    return pl.pallas_call(
        _attn_prompt_kernel,
        grid=(batch, D_ATTN // LANES, seq // SPAN),
        in_specs=[blk, blk, blk],
        out_specs=blk,
        out_shape=jax.ShapeDtypeStruct((batch, seq, D_ATTN), F32),
        scratch_shapes=scratch,
        compiler_params=_params(3),
        name="attn_prompt",
    )(q, k, v)


def _group_count(dist):
    total = None
    for d in DILATIONS:
        sh = d.bit_length() - 1
        hit = ((dist & (d - 1)) == 0) & ((dist >> sh) <= NEIGHBOURS) & (dist >= 0)
        total = hit.astype(F32) if total is None else total + hit.astype(F32)
    return total


def _sample_probs(q_ref, kn_ref, kc_ref, pc_ref, pn_ref):
    qi = lax.broadcasted_iota(jnp.int32, (DEC_SEQ, CACHE_LEN), 0)
    pos = lax.broadcasted_iota(jnp.int32, (DEC_SEQ, CACHE_LEN), 1)
    cnt_c = _group_count(CACHE_LEN + qi - pos)
    qn = lax.broadcasted_iota(jnp.int32, (DEC_SEQ, DEC_SEQ), 0)
    kn = lax.broadcasted_iota(jnp.int32, (DEC_SEQ, DEC_SEQ), 1)
    cnt_n = _group_count(qn - kn)
    for h in range(N_HEADS):
        qh = (q_ref[h] * SCALE).astype(BF16)
        s_c = jnp.where(cnt_c > 0, _dot(qh, kc_ref[h].astype(BF16)), NEG_INF)
        s_n = jnp.where(cnt_n > 0, _dot_nt(qh, kn_ref[h].astype(BF16)), NEG_INF)
        m = jnp.maximum(jnp.max(s_c, axis=-1, keepdims=True), jnp.max(s_n, axis=-1, keepdims=True))
        w_c = cnt_c * jnp.exp(s_c - m)
        w_n = cnt_n * jnp.exp(s_n - m)
        den = jnp.sum(w_c, axis=-1, keepdims=True) + jnp.sum(w_n, axis=-1, keepdims=True)
        pc_ref[h] = (w_c / den).astype(BF16)
        pn_ref[h] = w_n / den


def _sample_values(pc_ref, pn_ref, vc_ref, vn_ref, o_ref):
    for h in range(N_HEADS):
        o_ref[h] = (_dot_nt(pc_ref[h], vc_ref[h].astype(BF16))
                    + _dot(pn_ref[h].astype(BF16), vn_ref[h].astype(BF16)))


def _per_sequence_spec(shape, steps_per_batch):
    zeros = (0,) * len(shape)
    return pl.BlockSpec((None,) + shape, lambda b, j: (b * steps_per_batch + j,) + zeros)


FFN_HALO = 8


def _mix_ffn(x_ref, a_ref, o_ref, wo_ref, g2_ref, wg_ref, wu_ref, cw_ref, wd_ref, gf_ref, y_ref,
             x1_ref, h_ref, act_ref, conv):
    x1 = x_ref[...] + _dot(a_ref[...].astype(BF16), wo_ref[0:D_CONV, :])
    x1 = x1 + _dot(o_ref[...].astype(BF16), wo_ref[D_CONV:, :])
    x1_ref[...] = x1
    h_ref[...] = _rms_norm(x1, g2_ref[...]).astype(BF16)
    for c in range(N_FF_CHUNKS):
        cols = slice(c * FF_CHUNK, (c + 1) * FF_CHUNK)
        h = h_ref[...]
        g = conv(c, cols, _dot(h, wg_ref[:, cols]), cw_ref[:, cols])
        act_ref[:, cols] = ((g * jax.nn.sigmoid(g)) * _dot(h, wu_ref[:, cols])).astype(BF16)
    y_ref[...] = _rms_norm(x1_ref[...] + _dot(act_ref[...], wd_ref[...]), gf_ref[...])


def _conv_taps(cw, now, back1, back2):
    return cw[2:3, :] * now + cw[1:2, :] * back1 + cw[0:1, :] * back2 + cw[3:4, :]


def _ffn_prompt_kernel(x_ref, a_ref, o_ref, wo_ref, g2_ref, wg_ref, wu_ref, cw_ref, wd_ref, gf_ref,
                       spc_ref, spn_ref, svc_ref, svn_ref,
                       y_ref, st_ref, so_ref, x1_ref, h_ref, act_ref, gbuf_ref, carry_ref):
    @pl.when(pl.program_id(1) == 0)
    def _():
        carry_ref[...] = jnp.zeros(carry_ref.shape, F32)

    _sample_values(spc_ref, spn_ref, svc_ref, svn_ref, so_ref)

    def conv(c, cols, gp, cw):
        buf = gbuf_ref.at[c % 2]
        buf[0:FFN_HALO, :] = carry_ref[:, cols]
        buf[FFN_HALO:, :] = gp
        tail = gp[ROW_TILE - FFN_HALO:, :]
        carry_ref[:, cols] = tail
        st_ref[:, cols] = tail
        return _conv_taps(cw, gp, buf[FFN_HALO - 1:FFN_HALO - 1 + ROW_TILE, :],
                          buf[FFN_HALO - 2:FFN_HALO - 2 + ROW_TILE, :])

    _mix_ffn(x_ref, a_ref, o_ref, wo_ref, g2_ref, wg_ref, wu_ref, cw_ref, wd_ref, gf_ref, y_ref,
             x1_ref, h_ref, act_ref, conv)


def _ffn_weight_specs():
    return [_const_spec((D_MODEL, D_MODEL)), _const_spec((1, D_MODEL)),
            _const_spec((D_MODEL, D_FF)), _const_spec((D_MODEL, D_FF)),
            _const_spec((8, D_FF)), _const_spec((D_FF, D_MODEL)), _const_spec((1, D_MODEL))]


def _ffn_scratch(rows):
    return [pltpu.VMEM((rows, D_MODEL), F32), pltpu.VMEM((rows, D_MODEL), BF16),
            pltpu.VMEM((rows, D_FF), BF16)]


def _ffn_prompt(x, a, o, wo, g2, wg, wu, cw, wd, gf, pc, pn, vc_t, vn4):
    batch, seq, _ = x.shape
    nseq = pc.shape[0]
    steps = seq // ROW_TILE
    row = lambda width: pl.BlockSpec((None, ROW_TILE, width), lambda b, j: (b, j, 0))
    st = pl.BlockSpec((None, FFN_HALO, D_FF), lambda b, j: (b, 0, 0))
    new = _per_sequence_spec((N_HEADS, DEC_SEQ, HEAD_DIM), steps)
    return pl.pallas_call(
        _ffn_prompt_kernel,
        grid=(batch, steps),
        in_specs=[row(D_MODEL), row(D_CONV), row(D_ATTN)] + _ffn_weight_specs()
        + [_per_sequence_spec((N_HEADS, DEC_SEQ, CACHE_LEN), steps),
           _per_sequence_spec((N_HEADS, DEC_SEQ, DEC_SEQ), steps),
           _per_sequence_spec((N_HEADS, HEAD_DIM, CACHE_LEN), steps), new],
        out_specs=[row(D_MODEL), st, new],
        out_shape=[jax.ShapeDtypeStruct((batch, seq, D_MODEL), F32),
                   jax.ShapeDtypeStruct((batch, FFN_HALO, D_FF), F32),
                   jax.ShapeDtypeStruct((nseq, N_HEADS, DEC_SEQ, HEAD_DIM), F32)],
        scratch_shapes=_ffn_scratch(ROW_TILE)
        + [pltpu.VMEM((2, FFN_HALO + ROW_TILE, FF_CHUNK), F32), pltpu.VMEM((FFN_HALO, D_FF), F32)],
        compiler_params=_params(2),
        name="ffn_prompt",
    )(x, a, o, wo, g2, wg, wu, cw, wd, gf, pc, pn, vc_t, vn4)


def _ffn_sample_kernel(x_ref, a_ref, o_ref, past_ref, wo_ref, g2_ref, wg_ref, wu_ref, cw_ref, wd_ref,
                       gf_ref, y_ref, st_ref, x1_ref, h_ref, act_ref, gbuf_ref):
    rows = x_ref.shape[0]
    nseq = rows // DEC_SEQ

    def conv(c, cols, gp, cw):
        buf = gbuf_ref.at[c % 2]
        buf[0:2 * nseq, :] = past_ref[:, cols]
        buf[2 * nseq:, :] = gp
        st_ref[:, cols] = gp[rows - 2 * nseq:, :]
        return _conv_taps(cw, gp, buf[nseq:nseq + rows, :], buf[0:rows, :])

    _mix_ffn(x_ref, a_ref, o_ref, wo_ref, g2_ref, wg_ref, wu_ref, cw_ref, wd_ref, gf_ref, y_ref,
             x1_ref, h_ref, act_ref, conv)


def _ffn_sample(x, a, o, past, wo, g2, wg, wu, cw, wd, gf):
    rows = x.shape[0]
    nseq = rows // DEC_SEQ
    return pl.pallas_call(
        _ffn_sample_kernel,
        out_shape=[jax.ShapeDtypeStruct((rows, D_MODEL), F32),
                   jax.ShapeDtypeStruct((2 * nseq, D_FF), F32)],
        scratch_shapes=_ffn_scratch(rows) + [pltpu.VMEM((2, 2 * nseq + rows, FF_CHUNK), F32)],
        compiler_params=pltpu.CompilerParams(vmem_limit_bytes=VMEM_LIMIT),
        name="ffn_sample",
    )(x, a, o, past, wo, g2, wg, wu, cw, wd, gf)


def kernel(x_prompt, x_sample, cache_k_win, cache_v_win, state_conv_a, state_conv_ffn, norm_mix_g, w_in, conv_a_w, conv_a_b, ln_a_g, ln_a_b, w_out, norm_ffn_g, w_ffn_gate, w_ffn_up, ffn_conv_w, ffn_conv_b, w_ffn_down, norm_final_g):
    batch, seq, _ = x_prompt.shape
    nseq, dec_seq, _ = x_sample.shape
    assert w_in.shape[0] == 1, "single trunk layer"
    assert dec_seq == DEC_SEQ and cache_k_win.shape[2] == CACHE_LEN
    assert seq % SPAN == 0 and seq >= CACHE_LEN and (nseq * dec_seq) % ROW_TILE == 0
    assert batch * (seq // ROW_TILE) == nseq, "one sample sequence per prompt row tile"

    g1 = norm_mix_g
    w_in_b = w_in[0].astype(BF16)
    wkt_b = w_in_b[:, K_COLS[0]:K_COLS[1]].T
    wvt_b = w_in_b[:, V_COLS[0]:V_COLS[1]].T
    ca = (conv_a_w[0], conv_a_b, ln_a_g, ln_a_b)
    cw = jnp.concatenate([ffn_conv_w[0], ffn_conv_b, jnp.zeros((4, D_FF), F32)], axis=0)
    ffn_w = (w_out[0].astype(BF16), norm_ffn_g, w_ffn_gate[0].astype(BF16), w_ffn_up[0].astype(BF16),
             cw, w_ffn_down[0].astype(BF16), norm_final_g[None, :])

    rows = dec_seq * nseq
    xs = x_sample.transpose(1, 0, 2).reshape(rows, D_MODEL)
    glu_s, q_s, k_s, v_s = _in_proj_sample(xs, g1, w_in_b)
    a_s, st_a = _conv_a_sample(state_conv_a[0].transpose(1, 0, 2),
                               glu_s.reshape(dec_seq, nseq, D_CONV), *ca)
    per_head = lambda t: t.reshape(dec_seq, nseq, N_HEADS, HEAD_DIM).transpose(1, 2, 0, 3)
    cache_t = lambda t: t[0].transpose(0, 2, 3, 1)

    a_p, q_p, k_p, v_p, kt_win, vt_win, glu_tail, pc, pn = _in_proj_prompt(
        x_prompt, g1, w_in_b, wkt_b, wvt_b, *ca, per_head(q_s), per_head(k_s), cache_t(cache_k_win))
    o_p = _attn_prompt(q_p, k_p, v_p)
    y_p, st_p, o4 = _ffn_prompt(x_prompt, a_p, o_p, *ffn_w, pc, pn, cache_t(cache_v_win),
                                per_head(v_s))

    window = lambda t: t.reshape(batch, N_HEADS, HEAD_DIM, CACHE_LEN).transpose(0, 3, 1, 2)[None]
    conv_a_prompt = glu_tail[:, CONV_HALO - (CONV_A_WIDTH - 1):][None]
    conv_ffn_prompt = st_p[:, FFN_HALO - 2:][None]

    o_s = o4.transpose(2, 0, 1, 3).reshape(rows, D_ATTN)
    past_f = state_conv_ffn[0].transpose(1, 0, 2).reshape(2 * nseq, D_FF)
    y_s, st_f = _ffn_sample(xs, a_s.reshape(rows, D_CONV), o_s, past_f, *ffn_w)

    y_sample = y_s.reshape(dec_seq, nseq, D_MODEL).transpose(1, 0, 2)
    new_rows = lambda t: t.reshape(dec_seq, nseq, N_HEADS, HEAD_DIM).transpose(1, 0, 2, 3)[None]
    conv_a_sample = st_a.transpose(1, 0, 2)[None]
    conv_ffn_sample = st_f.reshape(2, nseq, D_FF).transpose(1, 0, 2)[None]

    return (y_p, y_sample, window(kt_win), window(vt_win), conv_a_prompt, conv_ffn_prompt,
            new_rows(k_s), new_rows(v_s), conv_a_sample, conv_ffn_sample)
```

```python
import functools

import jax
import jax.numpy as jnp
from jax import lax
from jax.experimental import pallas as pl
from jax.experimental.pallas import tpu as pltpu

D_MODEL = 1024
D_CONV = 256
D_ATTN = 768
HEAD_DIM = 64
N_HEADS = 12
D_IN = 2 * D_CONV + 3 * D_ATTN
CONV_A_WIDTH = 31
D_FF = 2816
EPS = 1e-6
NEG_INF = -1e30
SCALE = HEAD_DIM ** -0.5

DILATIONS = (1, 4, 16)
NEIGHBOURS = 128
QBLOCK = 128
SPAN = DILATIONS[-1] * QBLOCK
CACHE_LEN = 2048
DEC_SEQ = 8

LANES = 128
FF_CHUNK = 256
N_FF_CHUNKS = D_FF // FF_CHUNK
ROW_TILE = 256
VMEM_LIMIT = 56 * 1024 * 1024

F32 = jnp.float32
BF16 = jnp.bfloat16


def _const_spec(shape):
    nd = len(shape)
    return pl.BlockSpec(shape, lambda *_: (0,) * nd, pipeline_mode=pl.Buffered(1))


def _params(n_axes):
    return pltpu.CompilerParams(
        dimension_semantics=("arbitrary",) * n_axes, vmem_limit_bytes=VMEM_LIMIT)


def _rms_norm(x, g):
    ms = jnp.mean(x * x, axis=-1, keepdims=True)
    return x * lax.rsqrt(ms + EPS) * g


def _dot(a, b):
    return jnp.dot(a, b, preferred_element_type=F32)


def _dot_nt(a, b):
    return lax.dot_general(a, b, (((1,), (1,)), ((), ())), preferred_element_type=F32)


K_COLS = (2 * D_CONV + D_ATTN, 2 * D_CONV + 2 * D_ATTN)
V_COLS = (2 * D_CONV + 2 * D_ATTN, D_IN)


def _normed(x_ref, g_ref):
    return _rms_norm(x_ref[...], g_ref[...]).astype(BF16)


def _project_glu(h, w_ref, glu_ref):
    a_val = _dot(h, w_ref[:, 0:D_CONV])
    a_gate = _dot(h, w_ref[:, D_CONV:2 * D_CONV])
    glu_ref[...] = a_val * jax.nn.sigmoid(a_gate)


def _project_qkv(h, w_ref, q_ref, k_ref, v_ref):
    q_ref[...] = _dot(h, w_ref[:, 2 * D_CONV:K_COLS[0]])
    k_ref[...] = _dot(h, w_ref[:, K_COLS[0]:K_COLS[1]])
    v_ref[...] = _dot(h, w_ref[:, V_COLS[0]:V_COLS[1]])


def _in_proj_sample_kernel(x_ref, g_ref, w_ref, glu_ref, q_ref, k_ref, v_ref):
    h = _normed(x_ref, g_ref)
    _project_glu(h, w_ref, glu_ref)
    _project_qkv(h, w_ref, q_ref, k_ref, v_ref)


def _in_proj_prompt_kernel(x_ref, g_ref, w_ref, wkt_ref, wvt_ref, cw_ref, cb_ref, lg_ref, lb_ref,
                           sq_ref, skn_ref, skc_ref,
                           a_ref, q_ref, k_ref, v_ref, kt_ref, vt_ref, tail_ref, spc_ref, spn_ref,
                           win_ref, sh_ref, *, first_window_tile):
    @pl.when(pl.program_id(1) == 0)
    def _():
        win_ref[0:CONV_HALO, :] = jnp.zeros((CONV_HALO, D_CONV), F32)

    @pl.when(pl.program_id(1) > 0)
    def _():
        win_ref[0:CONV_HALO, :] = win_ref[ROW_TILE:ROW_TILE + CONV_HALO, :]

    h = _normed(x_ref, g_ref)
    _project_glu(h, w_ref, win_ref.at[CONV_HALO:CONV_HALO + ROW_TILE])
    tail_ref[...] = win_ref[ROW_TILE:ROW_TILE + CONV_HALO, :]

    first = CONV_HALO - (CONV_A_WIDTH - 1)
    for s in range(1, 8):
        sh_ref[s - 1] = win_ref[s:s + CONV_SHIFTED_ROWS, :]
    for c in range(ROW_TILE // CONV_ROWS):
        r0 = c * CONV_ROWS
        acc = jnp.broadcast_to(cb_ref[...], (CONV_ROWS, D_CONV))
        for j in range(CONV_A_WIDTH):
            whole, s = divmod(first + j, 8)
            src = win_ref if s == 0 else sh_ref.at[s - 1]
            acc = acc + cw_ref[j:j + 1, :] * src[r0 + 8 * whole:r0 + 8 * whole + CONV_ROWS, :]
        a_ref[r0:r0 + CONV_ROWS, :] = _layer_norm_silu(acc, lg_ref[...], lb_ref[...])

    _project_qkv(h, w_ref, q_ref, k_ref, v_ref)
    _sample_probs(sq_ref, skn_ref, skc_ref, spc_ref, spn_ref)

    @pl.when(pl.program_id(1) >= first_window_tile)
    def _():
        kt_ref[...] = _dot_nt(wkt_ref[...], h)
        vt_ref[...] = _dot_nt(wvt_ref[...], h)


def _in_proj_out_shapes(lead):
    return ([jax.ShapeDtypeStruct(lead + (D_CONV,), F32)]
            + [jax.ShapeDtypeStruct(lead + (D_ATTN,), F32)] * 3)


def _in_proj_sample(x2d, g, w_b):
    rows = x2d.shape[0]
    row = lambda width: pl.BlockSpec((ROW_TILE, width), lambda i: (i, 0))
    return pl.pallas_call(
        _in_proj_sample_kernel,
        grid=(rows // ROW_TILE,),
        in_specs=[row(D_MODEL), _const_spec((1, D_MODEL)), _const_spec((D_MODEL, D_IN))],
        out_specs=[row(D_CONV), row(D_ATTN), row(D_ATTN), row(D_ATTN)],
        out_shape=_in_proj_out_shapes((rows,)),
        compiler_params=_params(1),
        name="in_proj_sample",
    )(x2d, g, w_b)


def _in_proj_prompt(x3d, g, w_b, wkt_b, wvt_b, cw, cb, lg, lb, q4, kn4, kc_t):
    batch, seq, _ = x3d.shape
    nseq = q4.shape[0]
    steps = seq // ROW_TILE
    first = (seq - CACHE_LEN) // ROW_TILE
    row = lambda width: pl.BlockSpec((None, ROW_TILE, width), lambda b, j: (b, j, 0))
    win = pl.BlockSpec((None, D_ATTN, ROW_TILE), lambda b, j: (b, 0, jnp.maximum(j - first, 0)))
    tail = pl.BlockSpec((None, CONV_HALO, D_CONV), lambda b, j: (b, 0, 0))
    new = _per_sequence_spec(SAMPLE_NEW_SHAPE, steps)
    return pl.pallas_call(
        functools.partial(_in_proj_prompt_kernel, first_window_tile=first),
        grid=(batch, steps),
        in_specs=[row(D_MODEL), _const_spec((1, D_MODEL)), _const_spec((D_MODEL, D_IN)),
                  _const_spec((D_ATTN, D_MODEL)), _const_spec((D_ATTN, D_MODEL)),
                  _const_spec((CONV_A_WIDTH, D_CONV))] + [_const_spec((1, D_CONV))] * 3
        + [new, new, _per_sequence_spec(SAMPLE_CACHE_SHAPE, steps)],
        out_specs=[row(D_CONV), row(D_ATTN), row(D_ATTN), row(D_ATTN), win, win, tail,
                   _per_sequence_spec(SAMPLE_PC_SHAPE, steps),
                   _per_sequence_spec(SAMPLE_PN_SHAPE, steps)],
        out_shape=_in_proj_out_shapes((batch, seq))
        + [jax.ShapeDtypeStruct((batch, D_ATTN, CACHE_LEN), F32)] * 2
        + [jax.ShapeDtypeStruct((batch, CONV_HALO, D_CONV), F32),
           jax.ShapeDtypeStruct((nseq,) + SAMPLE_PC_SHAPE, BF16),
           jax.ShapeDtypeStruct((nseq,) + SAMPLE_PN_SHAPE, F32)],
        scratch_shapes=[pltpu.VMEM((CONV_HALO + ROW_TILE, D_CONV), F32),
                        pltpu.VMEM((7, CONV_SHIFTED_ROWS, D_CONV), F32)],
        compiler_params=_params(2),
        name="in_proj_prompt",
    )(x3d, g, w_b, wkt_b, wvt_b, cw, cb, lg, lb, q4, kn4, kc_t)


def _layer_norm_silu(y, g, b):
    mu = jnp.mean(y, axis=-1, keepdims=True)
    yc = y - mu
    z = yc * lax.rsqrt(jnp.mean(yc * yc, axis=-1, keepdims=True) + EPS) * g + b
    return z * jax.nn.sigmoid(z)


CONV_HALO = 32
CONV_ROWS = 64
CONV_SHIFTED_ROWS = ROW_TILE + CONV_HALO - 8


def _conv_a_sample_kernel(past_ref, glu_ref, w_ref, b_ref, lg_ref, lb_ref, a_ref, state_ref):
    n_past = CONV_A_WIDTH - 1

    def row(u):
        return past_ref[u] if u < n_past else glu_ref[u - n_past]

    for t in range(DEC_SEQ):
        acc = jnp.broadcast_to(b_ref[...], a_ref.shape[1:])
        for j in range(CONV_A_WIDTH):
            acc = acc + w_ref[j:j + 1, :] * row(t + j)
        a_ref[t] = _layer_norm_silu(acc, lg_ref[...], lb_ref[...])
    for u in range(n_past):
        state_ref[u] = row(u + DEC_SEQ)


def _conv_a_sample(past_t, glu_t, w, b, lg, lb):
    n_past, nseq, _ = past_t.shape
    return pl.pallas_call(
        _conv_a_sample_kernel,
        out_shape=[jax.ShapeDtypeStruct((DEC_SEQ, nseq, D_CONV), F32),
                   jax.ShapeDtypeStruct((n_past, nseq, D_CONV), F32)],
        compiler_params=pltpu.CompilerParams(vmem_limit_bytes=VMEM_LIMIT),
        name="conv_a_sample",
    )(past_t, glu_t, w, b, lg, lb)


def _group_geometry(d):
    nb = SPAN // (d * QBLOCK)
    slot = (nb + 1) * QBLOCK
    return nb, slot


def _attn_prompt_kernel(q_ref, k_ref, v_ref, o_ref, *scratch):
    n_g = len(DILATIONS)
    qg = scratch[0:n_g]
    kg = scratch[n_g:2 * n_g]
    vg = scratch[2 * n_g:3 * n_g]
    acc_n = scratch[3 * n_g:4 * n_g]
    den_n = scratch[4 * n_g:5 * n_g]
    max_n = scratch[5 * n_g:6 * n_g]
    stage_ref, s_ref, p_ref = scratch[6 * n_g:]
    sp = pl.program_id(2)
    lane = lax.broadcasted_iota(jnp.int32, (QBLOCK, LANES), 1)
    head0 = lane < HEAD_DIM

    @pl.when(sp == 0)
    def _():
        for g, d in enumerate(DILATIONS):
            _, slot = _group_geometry(d)
            vg[g][:, LANES:] = jnp.ones((d * slot, LANES), BF16)
            for r in range(d):
                kg[g][r * slot:r * slot + QBLOCK, :] = jnp.zeros((QBLOCK, LANES), BF16)
                vg[g][r * slot:r * slot + QBLOCK, 0:LANES] = jnp.zeros((QBLOCK, LANES), BF16)

    @pl.when(sp > 0)
    def _():
        for g, d in enumerate(DILATIONS):
            nb, slot = _group_geometry(d)
            for r in range(d):
                lo = r * slot
                kg[g][lo:lo + QBLOCK, :] = kg[g][lo + nb * QBLOCK:lo + slot, :]
                vg[g][lo:lo + QBLOCK, :] = vg[g][lo + nb * QBLOCK:lo + slot, :]

    def put(g, r, n, rows_q, x_q, x_k, x_v):
        _, slot = _group_geometry(DILATIONS[g])
        qg[g][rows_q, :] = (x_q * SCALE).astype(BF16)
        lo = r * slot + QBLOCK
        kg[g][lo:lo + n, :] = x_k.astype(BF16)
        vg[g][lo:lo + n, 0:LANES] = x_v.astype(BF16)

    put(0, 0, SPAN, slice(0, SPAN), q_ref[...], k_ref[...], v_ref[...])
    d1, d2 = DILATIONS[1], DILATIONS[2] // DILATIONS[1]
    n1 = SPAN // d1
    for r in range(d1):
        xs = [ref[pl.ds(r, n1, stride=d1), :] for ref in (q_ref, k_ref, v_ref)]
        for t, x in enumerate(xs):
            stage_ref[t, r * n1:(r + 1) * n1, :] = x
        put(1, r, n1, slice(r * n1, (r + 1) * n1), *xs)
    n2 = n1 // d2
    for r in range(DILATIONS[2]):
        rows = pl.ds((r % d1) * n1 + r // d1, n2, stride=d2)
        xs = [stage_ref[t, rows, :] for t in range(3)]
        put(2, r, n2, slice(r * n2, (r + 1) * n2), *xs)

    qi = lax.broadcasted_iota(jnp.int32, (QBLOCK, 2 * QBLOCK), 0)
    kc = lax.broadcasted_iota(jnp.int32, (QBLOCK, 2 * QBLOCK), 1)
    band = (kc >= qi) & (kc <= qi + NEIGHBOURS)

    def key_rows(u, nb):
        return pl.ds((u + u // nb) * QBLOCK, 2 * QBLOCK)

    def scores(u):
        zero = jnp.zeros((QBLOCK, LANES), BF16)
        for g, d in enumerate(DILATIONS):
            nb, _ = _group_geometry(d)
            kb = kg[g][key_rows(u, nb), :]
            qb = qg[g][pl.ds(u * QBLOCK, QBLOCK), :]
            s_ref[u % 2, 2 * g] = _dot_nt(jnp.where(head0, qb, zero), kb)
            s_ref[u % 2, 2 * g + 1] = _dot_nt(jnp.where(head0, zero, qb), kb)

    def result_rows(u, g):
        if g == 2:
            return pl.ds((u % d1) * n1 + u // d1, QBLOCK, stride=d2)
        return pl.ds(u * QBLOCK, QBLOCK)

    def softmax(u):
        for g, d in enumerate(DILATIONS):
            nb, _ = _group_geometry(d)
            first_key = jnp.where(sp * nb + u % nb == 0, QBLOCK, 0)
            mask = band & (kc >= first_key)
            ms = []
            for h in range(2):
                s = jnp.where(mask, s_ref[u % 2, 2 * g + h], NEG_INF)
                m = jnp.max(s, axis=-1, keepdims=True)
                p_ref[u % 2, 2 * g + h] = jnp.exp(s - m).astype(BF16)
                ms.append(m)
            max_n[g][result_rows(u, g), :] = jnp.where(head0, ms[0], ms[1])

    def values(u):
        for g, d in enumerate(DILATIONS):
            nb, _ = _group_geometry(d)
            vb = vg[g][key_rows(u, nb), :]
            e0 = _dot(p_ref[u % 2, 2 * g], vb)
            e1 = _dot(p_ref[u % 2, 2 * g + 1], vb)
            rows = result_rows(u, g)
            acc_n[g][rows, :] = jnp.where(head0, e0[:, :LANES], e1[:, :LANES])
            den_n[g][rows, :] = jnp.where(head0, e0[:, LANES:], e1[:, LANES:])

    n_units = SPAN // QBLOCK
    for u in range(n_units + 2):
        if u >= 2:
            values(u - 2)
        if 1 <= u <= n_units:
            softmax(u - 1)
        if u < n_units:
            scores(u)

    def merge(c, carry):
        major = pl.ds(pl.multiple_of(c * QBLOCK, QBLOCK), QBLOCK)
        by_pos = pl.ds(lax.bitwise_and(c, d1 - 1) * (QBLOCK * d1) + lax.shift_right_logical(c, 2),
                       QBLOCK, stride=d1)
        rows = (by_pos, major, major)
        ms = [max_n[g][rows[g], :] for g in range(n_g)]
        m_all = functools.reduce(jnp.maximum, ms)
        ws = [jnp.exp(m - m_all) for m in ms]
        num = sum(w * acc_n[g][rows[g], :] for g, w in enumerate(ws))
        den = sum(w * den_n[g][rows[g], :] for g, w in enumerate(ws))
        o_ref[by_pos, :] = num / den
        return carry

    lax.fori_loop(0, n_units, merge, 0)


def _attn_prompt(q, k, v):
    batch, seq, _ = q.shape
    blk = pl.BlockSpec((None, SPAN, LANES), lambda b, hp, s: (b, s, hp))
    n_g = len(DILATIONS)
    scratch = [pltpu.VMEM((SPAN, LANES), BF16) for _ in DILATIONS]
    scratch += [pltpu.VMEM((d * _group_geometry(d)[1], LANES), BF16) for d in DILATIONS]
    scratch += [pltpu.VMEM((d * _group_geometry(d)[1], 2 * LANES), BF16) for d in DILATIONS]
    scratch += [pltpu.VMEM((SPAN, LANES), F32) for _ in range(3 * n_g)]
    scratch += [pltpu.VMEM((3, SPAN, LANES), F32),
                pltpu.VMEM((2, 2 * n_g, QBLOCK, 2 * QBLOCK), F32),
                pltpu.VMEM((2, 2 * n_g, QBLOCK, 2 * QBLOCK), BF16)]
    return pl.pallas_call(
        _attn_prompt_kernel,
        grid=(batch, D_ATTN // LANES, seq // SPAN),
        in_specs=[blk, blk, blk],
        out_specs=blk,
        out_shape=jax.ShapeDtypeStruct((batch, seq, D_ATTN), F32),
        scratch_shapes=scratch,
        compiler_params=_params(3),
        name="attn_prompt",
    )(q, k, v)


def _group_count(dist):
    total = None
    for d in DILATIONS:
        sh = d.bit_length() - 1
        hit = ((dist & (d - 1)) == 0) & ((dist >> sh) <= NEIGHBOURS) & (dist >= 0)
        total = hit.astype(F32) if total is None else total + hit.astype(F32)
    return total


SAMPLE_NEW_SHAPE = (N_HEADS, DEC_SEQ, HEAD_DIM)
SAMPLE_CACHE_SHAPE = (N_HEADS, HEAD_DIM, CACHE_LEN)
SAMPLE_PC_SHAPE = (N_HEADS, DEC_SEQ, CACHE_LEN)
SAMPLE_PN_SHAPE = (N_HEADS, DEC_SEQ, DEC_SEQ)


def _sample_probs(q_ref, kn_ref, kc_ref, pc_ref, pn_ref):
    qi = lax.broadcasted_iota(jnp.int32, (DEC_SEQ, CACHE_LEN), 0)
    pos = lax.broadcasted_iota(jnp.int32, (DEC_SEQ, CACHE_LEN), 1)
    cnt_c = _group_count(CACHE_LEN + qi - pos)
    qn = lax.broadcasted_iota(jnp.int32, (DEC_SEQ, DEC_SEQ), 0)
    kn = lax.broadcasted_iota(jnp.int32, (DEC_SEQ, DEC_SEQ), 1)
    cnt_n = _group_count(qn - kn)
    for h in range(N_HEADS):
        qh = (q_ref[h] * SCALE).astype(BF16)
        s_c = jnp.where(cnt_c > 0, _dot(qh, kc_ref[h].astype(BF16)), NEG_INF)
        s_n = jnp.where(cnt_n > 0, _dot_nt(qh, kn_ref[h].astype(BF16)), NEG_INF)
        m = jnp.maximum(jnp.max(s_c, axis=-1, keepdims=True), jnp.max(s_n, axis=-1, keepdims=True))
        w_c = cnt_c * jnp.exp(s_c - m)
        w_n = cnt_n * jnp.exp(s_n - m)
        den = jnp.sum(w_c, axis=-1, keepdims=True) + jnp.sum(w_n, axis=-1, keepdims=True)
        pc_ref[h] = (w_c / den).astype(BF16)
        pn_ref[h] = w_n / den


def _sample_values(pc_ref, pn_ref, vc_ref, vn_ref, o_ref):
    for h in range(N_HEADS):
        o_ref[h] = (_dot_nt(pc_ref[h], vc_ref[h].astype(BF16))
                    + _dot(pn_ref[h].astype(BF16), vn_ref[h].astype(BF16)))


def _per_sequence_spec(shape, steps_per_batch):
    zeros = (0,) * len(shape)
    return pl.BlockSpec((None,) + shape, lambda b, j: (b * steps_per_batch + j,) + zeros)


FFN_HALO = 8


def _mix_ffn(x_ref, a_ref, o_ref, wo_ref, g2_ref, wg_ref, wu_ref, cw_ref, wd_ref, gf_ref, y_ref,
             x1_ref, h_ref, act_ref, conv):
    x1 = x_ref[...] + _dot(a_ref[...].astype(BF16), wo_ref[0:D_CONV, :])
    x1 = x1 + _dot(o_ref[...].astype(BF16), wo_ref[D_CONV:, :])
    x1_ref[...] = x1
    h_ref[...] = _rms_norm(x1, g2_ref[...]).astype(BF16)
    for c in range(N_FF_CHUNKS):
        cols = slice(c * FF_CHUNK, (c + 1) * FF_CHUNK)
        h = h_ref[...]
        g = conv(c, cols, _dot(h, wg_ref[:, cols]), cw_ref[:, cols])
        act_ref[:, cols] = ((g * jax.nn.sigmoid(g)) * _dot(h, wu_ref[:, cols])).astype(BF16)
    y_ref[...] = _rms_norm(x1_ref[...] + _dot(act_ref[...], wd_ref[...]), gf_ref[...])


def _conv_taps(cw, now, back1, back2):
    return cw[2:3, :] * now + cw[1:2, :] * back1 + cw[0:1, :] * back2 + cw[3:4, :]


def _ffn_prompt_kernel(x_ref, a_ref, o_ref, wo_ref, g2_ref, wg_ref, wu_ref, cw_ref, wd_ref, gf_ref,
                       spc_ref, spn_ref, svc_ref, svn_ref,
                       y_ref, st_ref, so_ref, x1_ref, h_ref, act_ref, gbuf_ref, carry_ref):
    @pl.when(pl.program_id(1) == 0)
    def _():
        carry_ref[...] = jnp.zeros(carry_ref.shape, F32)

    def conv(c, cols, gp, cw):
        buf = gbuf_ref.at[c % 2]
        buf[0:FFN_HALO, :] = carry_ref[:, cols]
        buf[FFN_HALO:, :] = gp
        tail = gp[ROW_TILE - FFN_HALO:, :]
        carry_ref[:, cols] = tail
        st_ref[:, cols] = tail
        return _conv_taps(cw, gp, buf[FFN_HALO - 1:FFN_HALO - 1 + ROW_TILE, :],
                          buf[FFN_HALO - 2:FFN_HALO - 2 + ROW_TILE, :])

    _mix_ffn(x_ref, a_ref, o_ref, wo_ref, g2_ref, wg_ref, wu_ref, cw_ref, wd_ref, gf_ref, y_ref,
             x1_ref, h_ref, act_ref, conv)
    _sample_values(spc_ref, spn_ref, svc_ref, svn_ref, so_ref)


def _ffn_weight_specs():
    return [_const_spec((D_MODEL, D_MODEL)), _const_spec((1, D_MODEL)),
            _const_spec((D_MODEL, D_FF)), _const_spec((D_MODEL, D_FF)),
            _const_spec((8, D_FF)), _const_spec((D_FF, D_MODEL)), _const_spec((1, D_MODEL))]


def _ffn_scratch(rows):
    return [pltpu.VMEM((rows, D_MODEL), F32), pltpu.VMEM((rows, D_MODEL), BF16),
            pltpu.VMEM((rows, D_FF), BF16)]


def _ffn_prompt(x, a, o, wo, g2, wg, wu, cw, wd, gf, pc, pn, vc_t, vn4):
    batch, seq, _ = x.shape
    nseq = pc.shape[0]
    steps = seq // ROW_TILE
    row = lambda width: pl.BlockSpec((None, ROW_TILE, width), lambda b, j: (b, j, 0))
    st = pl.BlockSpec((None, FFN_HALO, D_FF), lambda b, j: (b, 0, 0))
    new = _per_sequence_spec(SAMPLE_NEW_SHAPE, steps)
    return pl.pallas_call(
        _ffn_prompt_kernel,
        grid=(batch, steps),
        in_specs=[row(D_MODEL), row(D_CONV), row(D_ATTN)] + _ffn_weight_specs()
        + [_per_sequence_spec(SAMPLE_PC_SHAPE, steps),
           _per_sequence_spec(SAMPLE_PN_SHAPE, steps),
           _per_sequence_spec(SAMPLE_CACHE_SHAPE, steps), new],
        out_specs=[row(D_MODEL), st, new],
        out_shape=[jax.ShapeDtypeStruct((batch, seq, D_MODEL), F32),
                   jax.ShapeDtypeStruct((batch, FFN_HALO, D_FF), F32),
                   jax.ShapeDtypeStruct((nseq,) + SAMPLE_NEW_SHAPE, F32)],
        scratch_shapes=_ffn_scratch(ROW_TILE)
        + [pltpu.VMEM((2, FFN_HALO + ROW_TILE, FF_CHUNK), F32), pltpu.VMEM((FFN_HALO, D_FF), F32)],
        compiler_params=_params(2),
        name="ffn_prompt",
    )(x, a, o, wo, g2, wg, wu, cw, wd, gf, pc, pn, vc_t, vn4)


def _ffn_sample_kernel(x_ref, a_ref, o_ref, past_ref, wo_ref, g2_ref, wg_ref, wu_ref, cw_ref, wd_ref,
                       gf_ref, y_ref, st_ref, x1_ref, h_ref, act_ref, gbuf_ref):
    rows = x_ref.shape[0]
    nseq = rows // DEC_SEQ

    def conv(c, cols, gp, cw):
        buf = gbuf_ref.at[c % 2]
        buf[0:2 * nseq, :] = past_ref[:, cols]
        buf[2 * nseq:, :] = gp
        st_ref[:, cols] = gp[rows - 2 * nseq:, :]
        return _conv_taps(cw, gp, buf[nseq:nseq + rows, :], buf[0:rows, :])

    _mix_ffn(x_ref, a_ref, o_ref, wo_ref, g2_ref, wg_ref, wu_ref, cw_ref, wd_ref, gf_ref, y_ref,
             x1_ref, h_ref, act_ref, conv)


def _ffn_sample(x, a, o, past, wo, g2, wg, wu, cw, wd, gf):
    rows = x.shape[0]
    nseq = rows // DEC_SEQ
    return pl.pallas_call(
        _ffn_sample_kernel,
        out_shape=[jax.ShapeDtypeStruct((rows, D_MODEL), F32),
                   jax.ShapeDtypeStruct((2 * nseq, D_FF), F32)],
        scratch_shapes=_ffn_scratch(rows) + [pltpu.VMEM((2, 2 * nseq + rows, FF_CHUNK), F32)],
        compiler_params=pltpu.CompilerParams(vmem_limit_bytes=VMEM_LIMIT),
        name="ffn_sample",
    )(x, a, o, past, wo, g2, wg, wu, cw, wd, gf)


def kernel(x_prompt, x_sample, cache_k_win, cache_v_win, state_conv_a, state_conv_ffn, norm_mix_g, w_in, conv_a_w, conv_a_b, ln_a_g, ln_a_b, w_out, norm_ffn_g, w_ffn_gate, w_ffn_up, ffn_conv_w, ffn_conv_b, w_ffn_down, norm_final_g):
    batch, seq, _ = x_prompt.shape
    nseq, dec_seq, _ = x_sample.shape
    assert w_in.shape[0] == 1, "single trunk layer"
    assert dec_seq == DEC_SEQ and cache_k_win.shape[2] == CACHE_LEN
    assert seq % SPAN == 0 and seq >= CACHE_LEN and (nseq * dec_seq) % ROW_TILE == 0
    assert batch * (seq // ROW_TILE) == nseq, "one sample sequence per prompt row tile"

    g1 = norm_mix_g
    w_in_b = w_in[0].astype(BF16)
    wkt_b = w_in_b[:, K_COLS[0]:K_COLS[1]].T
    wvt_b = w_in_b[:, V_COLS[0]:V_COLS[1]].T
    ca = (conv_a_w[0], conv_a_b, ln_a_g, ln_a_b)
    cw = jnp.concatenate([ffn_conv_w[0], ffn_conv_b, jnp.zeros((4, D_FF), F32)], axis=0)
    ffn_w = (w_out[0].astype(BF16), norm_ffn_g, w_ffn_gate[0].astype(BF16), w_ffn_up[0].astype(BF16),
             cw, w_ffn_down[0].astype(BF16), norm_final_g[None, :])

    rows = dec_seq * nseq
    xs = x_sample.transpose(1, 0, 2).reshape(rows, D_MODEL)
    glu_s, q_s, k_s, v_s = _in_proj_sample(xs, g1, w_in_b)
    a_s, st_a = _conv_a_sample(state_conv_a[0].transpose(1, 0, 2),
                               glu_s.reshape(dec_seq, nseq, D_CONV), *ca)
    per_head = lambda t: t.reshape(dec_seq, nseq, N_HEADS, HEAD_DIM).transpose(1, 2, 0, 3)
    cache_t = lambda t: t[0].transpose(0, 2, 3, 1)

    a_p, q_p, k_p, v_p, kt_win, vt_win, glu_tail, pc, pn = _in_proj_prompt(
        x_prompt, g1, w_in_b, wkt_b, wvt_b, *ca, per_head(q_s), per_head(k_s), cache_t(cache_k_win))
    o_p = _attn_prompt(q_p, k_p, v_p)
    y_p, st_p, o4 = _ffn_prompt(x_prompt, a_p, o_p, *ffn_w, pc, pn, cache_t(cache_v_win),
                                per_head(v_s))

    window = lambda t: t.reshape(batch, N_HEADS, HEAD_DIM, CACHE_LEN).transpose(0, 3, 1, 2)[None]
    conv_a_prompt = glu_tail[:, CONV_HALO - (CONV_A_WIDTH - 1):][None]
    conv_ffn_prompt = st_p[:, FFN_HALO - 2:][None]

    o_s = o4.transpose(2, 0, 1, 3).reshape(rows, D_ATTN)
    past_f = state_conv_ffn[0].transpose(1, 0, 2).reshape(2 * nseq, D_FF)
    y_s, st_f = _ffn_sample(xs, a_s.reshape(rows, D_CONV), o_s, past_f, *ffn_w)

    y_sample = y_s.reshape(dec_seq, nseq, D_MODEL).transpose(1, 0, 2)
    new_rows = lambda t: t.reshape(dec_seq, nseq, N_HEADS, HEAD_DIM).transpose(1, 0, 2, 3)[None]
    conv_a_sample = st_a.transpose(1, 0, 2)[None]
    conv_ffn_sample = st_f.reshape(2, nseq, D_FF).transpose(1, 0, 2)[None]

    return (y_p, y_sample, window(kt_win), window(vt_win), conv_a_prompt, conv_ffn_prompt,
            new_rows(k_s), new_rows(v_s), conv_a_sample, conv_ffn_sample)
```

```python
import functools

import jax
import jax.numpy as jnp
from jax import lax
from jax.experimental import pallas as pl
from jax.experimental.pallas import tpu as pltpu

D_MODEL = 1024
D_CONV = 256
D_ATTN = 768
HEAD_DIM = 64
N_HEADS = 12
D_IN = 2 * D_CONV + 3 * D_ATTN
CONV_A_WIDTH = 31
D_FF = 2816
EPS = 1e-6
NEG_INF = -1e30
SCALE = HEAD_DIM ** -0.5
SCALE_LOG2E = SCALE * 1.4426950408889634

DILATIONS = (1, 4, 16)
NEIGHBOURS = 128
QBLOCK = 128
SPAN = DILATIONS[-1] * QBLOCK
CACHE_LEN = 2048
DEC_SEQ = 8

LANES = 128
FF_CHUNK = 256
N_FF_CHUNKS = D_FF // FF_CHUNK
ROW_TILE = 256
VMEM_LIMIT = 56 * 1024 * 1024

F32 = jnp.float32
BF16 = jnp.bfloat16


def _const_spec(shape):
    nd = len(shape)
    return pl.BlockSpec(shape, lambda *_: (0,) * nd, pipeline_mode=pl.Buffered(1))


def _params(n_axes):
    return pltpu.CompilerParams(
        dimension_semantics=("arbitrary",) * n_axes, vmem_limit_bytes=VMEM_LIMIT)


def _rms_norm(x, g):
    ms = jnp.mean(x * x, axis=-1, keepdims=True)
    return x * lax.rsqrt(ms + EPS) * g


def _dot(a, b):
    return jnp.dot(a, b, preferred_element_type=F32)


def _dot_nt(a, b):
    return lax.dot_general(a, b, (((1,), (1,)), ((), ())), preferred_element_type=F32)


K_COLS = (2 * D_CONV + D_ATTN, 2 * D_CONV + 2 * D_ATTN)
V_COLS = (2 * D_CONV + 2 * D_ATTN, D_IN)


def _normed(x_ref, g_ref):
    return _rms_norm(x_ref[...], g_ref[...]).astype(BF16)


def _project_glu(h, w_ref, glu_ref):
    a_val = _dot(h, w_ref[:, 0:D_CONV])
    a_gate = _dot(h, w_ref[:, D_CONV:2 * D_CONV])
    glu_ref[...] = a_val * jax.nn.sigmoid(a_gate)


def _project_qkv(h, w_ref, q_ref, k_ref, v_ref):
    q_ref[...] = _dot(h, w_ref[:, 2 * D_CONV:K_COLS[0]])
    k_ref[...] = _dot(h, w_ref[:, K_COLS[0]:K_COLS[1]])
    v_ref[...] = _dot(h, w_ref[:, V_COLS[0]:V_COLS[1]])


def _in_proj_sample_kernel(x_ref, g_ref, w_ref, glu_ref, q_ref, k_ref, v_ref):
    h = _normed(x_ref, g_ref)
    _project_glu(h, w_ref, glu_ref)
    _project_qkv(h, w_ref, q_ref, k_ref, v_ref)


def _in_proj_prompt_kernel(x_ref, g_ref, w_ref, wkt_ref, wvt_ref, cw_ref, cb_ref, lg_ref, lb_ref,
                           sq_ref, skn_ref, skc_ref,
                           a_ref, q_ref, k_ref, v_ref, kt_ref, vt_ref, tail_ref, spc_ref, spn_ref,
                           win_ref, sh_ref, *, first_window_tile):
    @pl.when(pl.program_id(1) == 0)
    def _():
        win_ref[0:CONV_HALO, :] = jnp.zeros((CONV_HALO, D_CONV), F32)

    @pl.when(pl.program_id(1) > 0)
    def _():
        win_ref[0:CONV_HALO, :] = win_ref[ROW_TILE:ROW_TILE + CONV_HALO, :]

    h = _normed(x_ref, g_ref)
    _project_glu(h, w_ref, win_ref.at[CONV_HALO:CONV_HALO + ROW_TILE])
    tail_ref[...] = win_ref[ROW_TILE:ROW_TILE + CONV_HALO, :]

    first = CONV_HALO - (CONV_A_WIDTH - 1)
    for s in range(1, 8):
        sh_ref[s - 1] = win_ref[s:s + CONV_SHIFTED_ROWS, :]
    for c in range(ROW_TILE // CONV_ROWS):
        r0 = c * CONV_ROWS
        acc = jnp.broadcast_to(cb_ref[...], (CONV_ROWS, D_CONV))
        for j in range(CONV_A_WIDTH):
            whole, s = divmod(first + j, 8)
            src = win_ref if s == 0 else sh_ref.at[s - 1]
            acc = acc + cw_ref[j:j + 1, :] * src[r0 + 8 * whole:r0 + 8 * whole + CONV_ROWS, :]
        a_ref[r0:r0 + CONV_ROWS, :] = _layer_norm_silu(acc, lg_ref[...], lb_ref[...])

    _project_qkv(h, w_ref, q_ref, k_ref, v_ref)
    _sample_probs(sq_ref, skn_ref, skc_ref, spc_ref, spn_ref, first_head=0)

    @pl.when(pl.program_id(1) >= first_window_tile)
    def _():
        kt_ref[...] = _dot_nt(wkt_ref[...], h)
        vt_ref[...] = _dot_nt(wvt_ref[...], h)


def _in_proj_out_shapes(lead):
    return ([jax.ShapeDtypeStruct(lead + (D_CONV,), F32)]
            + [jax.ShapeDtypeStruct(lead + (D_ATTN,), F32)] * 3)


def _in_proj_sample(x2d, g, w_b):
    rows = x2d.shape[0]
    row = lambda width: pl.BlockSpec((ROW_TILE, width), lambda i: (i, 0))
    return pl.pallas_call(
        _in_proj_sample_kernel,
        grid=(rows // ROW_TILE,),
        in_specs=[row(D_MODEL), _const_spec((1, D_MODEL)), _const_spec((D_MODEL, D_IN))],
        out_specs=[row(D_CONV), row(D_ATTN), row(D_ATTN), row(D_ATTN)],
        out_shape=_in_proj_out_shapes((rows,)),
        compiler_params=_params(1),
        name="in_proj_sample",
    )(x2d, g, w_b)


def _in_proj_prompt(x3d, g, w_b, wkt_b, wvt_b, cw, cb, lg, lb, q4, kn4, kc_t):
    batch, seq, _ = x3d.shape
    nseq = q4.shape[0]
    steps = seq // ROW_TILE
    first = (seq - CACHE_LEN) // ROW_TILE
    row = lambda width: pl.BlockSpec((None, ROW_TILE, width), lambda b, j: (b, j, 0))
    win = pl.BlockSpec((None, D_ATTN, ROW_TILE), lambda b, j: (b, 0, jnp.maximum(j - first, 0)))
    tail = pl.BlockSpec((None, CONV_HALO, D_CONV), lambda b, j: (b, 0, 0))
    new = _per_sequence_spec(SAMPLE_NEW_SHAPE, steps)
    return pl.pallas_call(
        functools.partial(_in_proj_prompt_kernel, first_window_tile=first),
        grid=(batch, steps),
        in_specs=[row(D_MODEL), _const_spec((1, D_MODEL)), _const_spec((D_MODEL, D_IN)),
                  _const_spec((D_ATTN, D_MODEL)), _const_spec((D_ATTN, D_MODEL)),
                  _const_spec((CONV_A_WIDTH, D_CONV))] + [_const_spec((1, D_CONV))] * 3
        + [new, new, _per_sequence_spec(SAMPLE_HALF_CACHE_SHAPE, steps)],
        out_specs=[row(D_CONV), row(D_ATTN), row(D_ATTN), row(D_ATTN), win, win, tail,
                   _per_sequence_spec(SAMPLE_PC_SHAPE, steps),
                   _per_sequence_spec(SAMPLE_PN_SHAPE, steps)],
        out_shape=_in_proj_out_shapes((batch, seq))
        + [jax.ShapeDtypeStruct((batch, D_ATTN, CACHE_LEN), F32)] * 2
        + [jax.ShapeDtypeStruct((batch, CONV_HALO, D_CONV), F32),
           jax.ShapeDtypeStruct((nseq,) + SAMPLE_PC_SHAPE, BF16),
           jax.ShapeDtypeStruct((nseq,) + SAMPLE_PN_SHAPE, F32)],
        scratch_shapes=[pltpu.VMEM((CONV_HALO + ROW_TILE, D_CONV), F32),
                        pltpu.VMEM((7, CONV_SHIFTED_ROWS, D_CONV), F32)],
        compiler_params=_params(2),
        name="in_proj_prompt",
    )(x3d, g, w_b, wkt_b, wvt_b, cw, cb, lg, lb, q4, kn4, kc_t)


def _layer_norm_silu(y, g, b):
    mu = jnp.mean(y, axis=-1, keepdims=True)
    yc = y - mu
    z = yc * lax.rsqrt(jnp.mean(yc * yc, axis=-1, keepdims=True) + EPS) * g + b
    return z * jax.nn.sigmoid(z)


CONV_HALO = 32
CONV_ROWS = 64
CONV_SHIFTED_ROWS = ROW_TILE + CONV_HALO - 8


def _conv_a_sample_kernel(past_ref, glu_ref, w_ref, b_ref, lg_ref, lb_ref, a_ref, state_ref):
    n_past = CONV_A_WIDTH - 1

    def row(u):
        return past_ref[u] if u < n_past else glu_ref[u - n_past]

    for t in range(DEC_SEQ):
        acc = jnp.broadcast_to(b_ref[...], a_ref.shape[1:])
        for j in range(CONV_A_WIDTH):
            acc = acc + w_ref[j:j + 1, :] * row(t + j)
        a_ref[t] = _layer_norm_silu(acc, lg_ref[...], lb_ref[...])
    for u in range(n_past):
        state_ref[u] = row(u + DEC_SEQ)


def _conv_a_sample(past_t, glu_t, w, b, lg, lb):
    n_past, nseq, _ = past_t.shape
    return pl.pallas_call(
        _conv_a_sample_kernel,
        out_shape=[jax.ShapeDtypeStruct((DEC_SEQ, nseq, D_CONV), F32),
                   jax.ShapeDtypeStruct((n_past, nseq, D_CONV), F32)],
        compiler_params=pltpu.CompilerParams(vmem_limit_bytes=VMEM_LIMIT),
        name="conv_a_sample",
    )(past_t, glu_t, w, b, lg, lb)


def _group_geometry(d):
    nb = SPAN // (d * QBLOCK)
    slot = (nb + 1) * QBLOCK
    return nb, slot


def _attn_prompt_kernel(q_ref, k_ref, v_ref, o_ref, *scratch):
    n_g = len(DILATIONS)
    qg = scratch[0:n_g]
    kg = scratch[n_g:2 * n_g]
    vg = scratch[2 * n_g:3 * n_g]
    acc_n = scratch[3 * n_g:4 * n_g]
    den_n = scratch[4 * n_g:5 * n_g]
    max_n = scratch[5 * n_g:6 * n_g]
    stage_ref, s_ref, p_ref, bias_ref = scratch[6 * n_g:]
    sp = pl.program_id(2)
    lane = lax.broadcasted_iota(jnp.int32, (QBLOCK, LANES), 1)
    head0 = lane < HEAD_DIM

    @pl.when(sp == 0)
    def _():
        for g, d in enumerate(DILATIONS):
            _, slot = _group_geometry(d)
            vg[g][:, LANES:] = jnp.ones((d * slot, LANES), BF16)
            for r in range(d):
                kg[g][r * slot:r * slot + QBLOCK, :] = jnp.zeros((QBLOCK, LANES), BF16)
                vg[g][r * slot:r * slot + QBLOCK, 0:LANES] = jnp.zeros((QBLOCK, LANES), BF16)

    @pl.when(sp > 0)
    def _():
        for g, d in enumerate(DILATIONS):
            nb, slot = _group_geometry(d)
            for r in range(d):
                lo = r * slot
                kg[g][lo:lo + QBLOCK, :] = kg[g][lo + nb * QBLOCK:lo + slot, :]
                vg[g][lo:lo + QBLOCK, :] = vg[g][lo + nb * QBLOCK:lo + slot, :]

    def put(g, r, n, rows_q, x_q, x_k, x_v):
        _, slot = _group_geometry(DILATIONS[g])
        qg[g][rows_q, :] = (x_q * SCALE_LOG2E).astype(BF16)
        lo = r * slot + QBLOCK
        kg[g][lo:lo + n, :] = x_k.astype(BF16)
        vg[g][lo:lo + n, 0:LANES] = x_v.astype(BF16)

    put(0, 0, SPAN, slice(0, SPAN), q_ref[...], k_ref[...], v_ref[...])
    d1, d2 = DILATIONS[1], DILATIONS[2] // DILATIONS[1]
    n1 = SPAN // d1
    for r in range(d1):
        xs = [ref[pl.ds(r, n1, stride=d1), :] for ref in (q_ref, k_ref, v_ref)]
        for t, x in enumerate(xs):
            stage_ref[t, r * n1:(r + 1) * n1, :] = x
        put(1, r, n1, slice(r * n1, (r + 1) * n1), *xs)
    n2 = n1 // d2
    for r in range(DILATIONS[2]):
        rows = pl.ds((r % d1) * n1 + r // d1, n2, stride=d2)
        xs = [stage_ref[t, rows, :] for t in range(3)]
        put(2, r, n2, slice(r * n2, (r + 1) * n2), *xs)

    qi = lax.broadcasted_iota(jnp.int32, (QBLOCK, 2 * QBLOCK), 0)
    kc = lax.broadcasted_iota(jnp.int32, (QBLOCK, 2 * QBLOCK), 1)
    band = (kc >= qi) & (kc <= qi + NEIGHBOURS)
    bias_ref[0] = jnp.where(band, 0.0, NEG_INF)
    bias_ref[1] = jnp.where(band & (kc >= QBLOCK), 0.0, NEG_INF)

    def key_rows(u, nb):
        return pl.ds((u + u // nb) * QBLOCK, 2 * QBLOCK)

    def scores(u):
        zero = jnp.zeros((QBLOCK, LANES), BF16)
        for g, d in enumerate(DILATIONS):
            nb, _ = _group_geometry(d)
            kb = kg[g][key_rows(u, nb), :]
            qb = qg[g][pl.ds(u * QBLOCK, QBLOCK), :]
            s_ref[u % 2, 2 * g] = _dot_nt(jnp.where(head0, qb, zero), kb)
            s_ref[u % 2, 2 * g + 1] = _dot_nt(jnp.where(head0, zero, qb), kb)

    def result_rows(u, g):
        if g == 2:
            return pl.ds((u % d1) * n1 + u // d1, QBLOCK, stride=d2)
        return pl.ds(u * QBLOCK, QBLOCK)

    def softmax(u):
        for g, d in enumerate(DILATIONS):
            nb, _ = _group_geometry(d)
            first = jnp.where(sp == 0, 1, 0) if u % nb == 0 else 0
            ms = []
            for h in range(2):
                s = s_ref[u % 2, 2 * g + h] + bias_ref[first]
                m = jnp.max(s, axis=-1, keepdims=True)
                p_ref[u % 2, 2 * g + h] = jnp.exp2(s - m).astype(BF16)
                ms.append(m)
            max_n[g][result_rows(u, g), :] = jnp.where(head0, ms[0], ms[1])

    def values(u):
        for g, d in enumerate(DILATIONS):
            nb, _ = _group_geometry(d)
            vb = vg[g][key_rows(u, nb), :]
            e0 = _dot(p_ref[u % 2, 2 * g], vb)
            e1 = _dot(p_ref[u % 2, 2 * g + 1], vb)
            rows = result_rows(u, g)
            acc_n[g][rows, :] = jnp.where(head0, e0[:, :LANES], e1[:, :LANES])
            den_n[g][rows, :] = jnp.where(head0, e0[:, LANES:], e1[:, LANES:])

    n_units = SPAN // QBLOCK
    for u in range(n_units + 2):
        if u >= 2:
            values(u - 2)
        if 1 <= u <= n_units:
            softmax(u - 1)
        if u < n_units:
            scores(u)

    def merge(c, carry):
        major = pl.ds(pl.multiple_of(c * QBLOCK, QBLOCK), QBLOCK)
        by_pos = pl.ds(lax.bitwise_and(c, d1 - 1) * (QBLOCK * d1) + lax.shift_right_logical(c, 2),
                       QBLOCK, stride=d1)
        rows = (by_pos, major, major)
        ms = [max_n[g][rows[g], :] for g in range(n_g)]
        m_all = functools.reduce(jnp.maximum, ms)
        ws = [jnp.exp2(m - m_all) for m in ms]
        num = sum(w * acc_n[g][rows[g], :] for g, w in enumerate(ws))
        den = sum(w * den_n[g][rows[g], :] for g, w in enumerate(ws))
        o_ref[by_pos, :] = num / den
        return carry

    lax.fori_loop(0, n_units, merge, 0)


def _attn_prompt(q, k, v):
    batch, seq, _ = q.shape
    blk = pl.BlockSpec((None, SPAN, LANES), lambda b, hp, s: (b, s, hp))
    n_g = len(DILATIONS)
    scratch = [pltpu.VMEM((SPAN, LANES), BF16) for _ in DILATIONS]
    scratch += [pltpu.VMEM((d * _group_geometry(d)[1], LANES), BF16) for d in DILATIONS]
    scratch += [pltpu.VMEM((d * _group_geometry(d)[1], 2 * LANES), BF16) for d in DILATIONS]
    scratch += [pltpu.VMEM((SPAN, LANES), F32) for _ in range(3 * n_g)]
    scratch += [pltpu.VMEM((3, SPAN, LANES), F32),
                pltpu.VMEM((2, 2 * n_g, QBLOCK, 2 * QBLOCK), F32),
                pltpu.VMEM((2, 2 * n_g, QBLOCK, 2 * QBLOCK), BF16),
                pltpu.VMEM((2, QBLOCK, 2 * QBLOCK), F32)]
    return pl.pallas_call(
        _attn_prompt_kernel,
        grid=(batch, D_ATTN // LANES, seq // SPAN),
        in_specs=[blk, blk, blk],
        out_specs=blk,
        out_shape=jax.ShapeDtypeStruct((batch, seq, D_ATTN), F32),
        scratch_shapes=scratch,
        compiler_params=_params(3),
        name="attn_prompt",
    )(q, k, v)


def _group_count(dist):
    total = None
    for d in DILATIONS:
        sh = d.bit_length() - 1
        hit = ((dist & (d - 1)) == 0) & ((dist >> sh) <= NEIGHBOURS) & (dist >= 0)
        total = hit.astype(F32) if total is None else total + hit.astype(F32)
    return total


EARLY_HEADS = N_HEADS // 2
SAMPLE_NEW_SHAPE = (N_HEADS, DEC_SEQ, HEAD_DIM)
SAMPLE_CACHE_SHAPE = (N_HEADS, HEAD_DIM, CACHE_LEN)
SAMPLE_HALF_CACHE_SHAPE = (EARLY_HEADS, HEAD_DIM, CACHE_LEN)
SAMPLE_PC_SHAPE = (EARLY_HEADS, DEC_SEQ, CACHE_LEN)
SAMPLE_PN_SHAPE = (EARLY_HEADS, DEC_SEQ, DEC_SEQ)


def _sample_probs(q_ref, kn_ref, kc_ref, pc_ref, pn_ref, first_head):
    qi = lax.broadcasted_iota(jnp.int32, (DEC_SEQ, CACHE_LEN), 0)
    pos = lax.broadcasted_iota(jnp.int32, (DEC_SEQ, CACHE_LEN), 1)
    cnt_c = _group_count(CACHE_LEN + qi - pos)
    qn = lax.broadcasted_iota(jnp.int32, (DEC_SEQ, DEC_SEQ), 0)
    kn = lax.broadcasted_iota(jnp.int32, (DEC_SEQ, DEC_SEQ), 1)
    cnt_n = _group_count(qn - kn)
    for i in range(kc_ref.shape[0]):
        h = first_head + i
        qh = (q_ref[h] * SCALE_LOG2E).astype(BF16)
        s_c = jnp.where(cnt_c > 0, _dot(qh, kc_ref[i].astype(BF16)), NEG_INF)
        s_n = jnp.where(cnt_n > 0, _dot_nt(qh, kn_ref[h].astype(BF16)), NEG_INF)
        m = jnp.maximum(jnp.max(s_c, axis=-1, keepdims=True), jnp.max(s_n, axis=-1, keepdims=True))
        w_c = cnt_c * jnp.exp2(s_c - m)
        w_n = cnt_n * jnp.exp2(s_n - m)
        den = jnp.sum(w_c, axis=-1, keepdims=True) + jnp.sum(w_n, axis=-1, keepdims=True)
        pc_ref[i] = (w_c / den).astype(BF16)
        pn_ref[i] = w_n / den


def _sample_values(pc_ref, pn_ref, vc_ref, vn_ref, o_ref, first_head):
    for i in range(pc_ref.shape[0]):
        h = first_head + i
        o_ref[h] = (_dot_nt(pc_ref[i], vc_ref[h].astype(BF16))
                    + _dot(pn_ref[i].astype(BF16), vn_ref[h].astype(BF16)))


def _per_sequence_spec(shape, steps_per_batch, head_block=0):
    rest = (head_block,) + (0,) * (len(shape) - 1)
    return pl.BlockSpec((None,) + shape, lambda b, j: (b * steps_per_batch + j,) + rest)


FFN_HALO = 8


def _mix_ffn(x_ref, a_ref, o_ref, wo_ref, g2_ref, wg_ref, wu_ref, cw_ref, wd_ref, gf_ref, y_ref,
             x1_ref, h_ref, act_ref, conv):
    x1 = x_ref[...] + _dot(a_ref[...].astype(BF16), wo_ref[0:D_CONV, :])
    x1 = x1 + _dot(o_ref[...].astype(BF16), wo_ref[D_CONV:, :])
    x1_ref[...] = x1
    h_ref[...] = _rms_norm(x1, g2_ref[...]).astype(BF16)
    for c in range(N_FF_CHUNKS):
        cols = slice(c * FF_CHUNK, (c + 1) * FF_CHUNK)
        h = h_ref[...]
        g = conv(c, cols, _dot(h, wg_ref[:, cols]), cw_ref[:, cols])
        act_ref[:, cols] = ((g * jax.nn.sigmoid(g)) * _dot(h, wu_ref[:, cols])).astype(BF16)
    y_ref[...] = _rms_norm(x1_ref[...] + _dot(act_ref[...], wd_ref[...]), gf_ref[...])


def _conv_taps(cw, now, back1, back2):
    return cw[2:3, :] * now + cw[1:2, :] * back1 + cw[0:1, :] * back2 + cw[3:4, :]


def _ffn_prompt_kernel(x_ref, a_ref, o_ref, wo_ref, g2_ref, wg_ref, wu_ref, cw_ref, wd_ref, gf_ref,
                       spc_ref, spn_ref, sq_ref, skn_ref, skc_ref, svc_ref, svn_ref,
                       y_ref, st_ref, so_ref, x1_ref, h_ref, act_ref, gbuf_ref, carry_ref,
                       spc_late_ref, spn_late_ref):
    @pl.when(pl.program_id(1) == 0)
    def _():
        carry_ref[...] = jnp.zeros(carry_ref.shape, F32)

    def conv(c, cols, gp, cw):
        buf = gbuf_ref.at[c % 2]
        buf[0:FFN_HALO, :] = carry_ref[:, cols]
        buf[FFN_HALO:, :] = gp
        tail = gp[ROW_TILE - FFN_HALO:, :]
        carry_ref[:, cols] = tail
        st_ref[:, cols] = tail
        return _conv_taps(cw, gp, buf[FFN_HALO - 1:FFN_HALO - 1 + ROW_TILE, :],
                          buf[FFN_HALO - 2:FFN_HALO - 2 + ROW_TILE, :])

    _mix_ffn(x_ref, a_ref, o_ref, wo_ref, g2_ref, wg_ref, wu_ref, cw_ref, wd_ref, gf_ref, y_ref,
             x1_ref, h_ref, act_ref, conv)
    _sample_values(spc_ref, spn_ref, svc_ref, svn_ref, so_ref, first_head=0)
    _sample_probs(sq_ref, skn_ref, skc_ref, spc_late_ref, spn_late_ref, first_head=EARLY_HEADS)
    _sample_values(spc_late_ref, spn_late_ref, svc_ref, svn_ref, so_ref, first_head=EARLY_HEADS)


def _ffn_weight_specs():
    return [_const_spec((D_MODEL, D_MODEL)), _const_spec((1, D_MODEL)),
            _const_spec((D_MODEL, D_FF)), _const_spec((D_MODEL, D_FF)),
            _const_spec((8, D_FF)), _const_spec((D_FF, D_MODEL)), _const_spec((1, D_MODEL))]


def _ffn_scratch(rows):
    return [pltpu.VMEM((rows, D_MODEL), F32), pltpu.VMEM((rows, D_MODEL), BF16),
            pltpu.VMEM((rows, D_FF), BF16)]


def _ffn_prompt(x, a, o, wo, g2, wg, wu, cw, wd, gf, pc, pn, q4, kn4, kc_t, vc_t, vn4):
    batch, seq, _ = x.shape
    nseq = pc.shape[0]
    steps = seq // ROW_TILE
    row = lambda width: pl.BlockSpec((None, ROW_TILE, width), lambda b, j: (b, j, 0))
    st = pl.BlockSpec((None, FFN_HALO, D_FF), lambda b, j: (b, 0, 0))
    new = _per_sequence_spec(SAMPLE_NEW_SHAPE, steps)
    return pl.pallas_call(
        _ffn_prompt_kernel,
        grid=(batch, steps),
        in_specs=[row(D_MODEL), row(D_CONV), row(D_ATTN)] + _ffn_weight_specs()
        + [_per_sequence_spec(SAMPLE_PC_SHAPE, steps),
           _per_sequence_spec(SAMPLE_PN_SHAPE, steps), new, new,
           _per_sequence_spec(SAMPLE_HALF_CACHE_SHAPE, steps, head_block=1),
           _per_sequence_spec(SAMPLE_CACHE_SHAPE, steps), new],
        out_specs=[row(D_MODEL), st, new],
        out_shape=[jax.ShapeDtypeStruct((batch, seq, D_MODEL), F32),
                   jax.ShapeDtypeStruct((batch, FFN_HALO, D_FF), F32),
                   jax.ShapeDtypeStruct((nseq,) + SAMPLE_NEW_SHAPE, F32)],
        scratch_shapes=_ffn_scratch(ROW_TILE)
        + [pltpu.VMEM((2, FFN_HALO + ROW_TILE, FF_CHUNK), F32), pltpu.VMEM((FFN_HALO, D_FF), F32),
           pltpu.VMEM(SAMPLE_PC_SHAPE, BF16), pltpu.VMEM(SAMPLE_PN_SHAPE, F32)],
        compiler_params=_params(2),
        name="ffn_prompt",
    )(x, a, o, wo, g2, wg, wu, cw, wd, gf, pc, pn, q4, kn4, kc_t, vc_t, vn4)


def _ffn_sample_kernel(x_ref, a_ref, o_ref, past_ref, wo_ref, g2_ref, wg_ref, wu_ref, cw_ref, wd_ref,
                       gf_ref, y_ref, st_ref, x1_ref, h_ref, act_ref, gbuf_ref):
    rows = x_ref.shape[0]
    nseq = rows // DEC_SEQ

    def conv(c, cols, gp, cw):
        buf = gbuf_ref.at[c % 2]
        buf[0:2 * nseq, :] = past_ref[:, cols]
        buf[2 * nseq:, :] = gp
        st_ref[:, cols] = gp[rows - 2 * nseq:, :]
        return _conv_taps(cw, gp, buf[nseq:nseq + rows, :], buf[0:rows, :])

    _mix_ffn(x_ref, a_ref, o_ref, wo_ref, g2_ref, wg_ref, wu_ref, cw_ref, wd_ref, gf_ref, y_ref,
             x1_ref, h_ref, act_ref, conv)


def _ffn_sample(x, a, o, past, wo, g2, wg, wu, cw, wd, gf):
    rows = x.shape[0]
    nseq = rows // DEC_SEQ
    return pl.pallas_call(
        _ffn_sample_kernel,
        out_shape=[jax.ShapeDtypeStruct((rows, D_MODEL), F32),
                   jax.ShapeDtypeStruct((2 * nseq, D_FF), F32)],
        scratch_shapes=_ffn_scratch(rows) + [pltpu.VMEM((2, 2 * nseq + rows, FF_CHUNK), F32)],
        compiler_params=pltpu.CompilerParams(vmem_limit_bytes=VMEM_LIMIT),
        name="ffn_sample",
    )(x, a, o, past, wo, g2, wg, wu, cw, wd, gf)


def kernel(x_prompt, x_sample, cache_k_win, cache_v_win, state_conv_a, state_conv_ffn, norm_mix_g, w_in, conv_a_w, conv_a_b, ln_a_g, ln_a_b, w_out, norm_ffn_g, w_ffn_gate, w_ffn_up, ffn_conv_w, ffn_conv_b, w_ffn_down, norm_final_g):
    batch, seq, _ = x_prompt.shape
    nseq, dec_seq, _ = x_sample.shape
    assert w_in.shape[0] == 1, "single trunk layer"
    assert dec_seq == DEC_SEQ and cache_k_win.shape[2] == CACHE_LEN
    assert seq % SPAN == 0 and seq >= CACHE_LEN and (nseq * dec_seq) % ROW_TILE == 0
    assert batch * (seq // ROW_TILE) == nseq, "one sample sequence per prompt row tile"

    g1 = norm_mix_g
    w_in_b = w_in[0].astype(BF16)
    wkt_b = w_in_b[:, K_COLS[0]:K_COLS[1]].T
    wvt_b = w_in_b[:, V_COLS[0]:V_COLS[1]].T
    ca = (conv_a_w[0], conv_a_b, ln_a_g, ln_a_b)
    cw = jnp.concatenate([ffn_conv_w[0], ffn_conv_b, jnp.zeros((4, D_FF), F32)], axis=0)
    ffn_w = (w_out[0].astype(BF16), norm_ffn_g, w_ffn_gate[0].astype(BF16), w_ffn_up[0].astype(BF16),
             cw, w_ffn_down[0].astype(BF16), norm_final_g[None, :])

    rows = dec_seq * nseq
    xs = x_sample.transpose(1, 0, 2).reshape(rows, D_MODEL)
    glu_s, q_s, k_s, v_s = _in_proj_sample(xs, g1, w_in_b)
    a_s, st_a = _conv_a_sample(state_conv_a[0].transpose(1, 0, 2),
                               glu_s.reshape(dec_seq, nseq, D_CONV), *ca)
    per_head = lambda t: t.reshape(dec_seq, nseq, N_HEADS, HEAD_DIM).transpose(1, 2, 0, 3)
    cache_t = lambda t: t[0].transpose(0, 2, 3, 1)

    a_p, q_p, k_p, v_p, kt_win, vt_win, glu_tail, pc, pn = _in_proj_prompt(
        x_prompt, g1, w_in_b, wkt_b, wvt_b, *ca, per_head(q_s), per_head(k_s), cache_t(cache_k_win))
    o_p = _attn_prompt(q_p, k_p, v_p)
    y_p, st_p, o4 = _ffn_prompt(x_prompt, a_p, o_p, *ffn_w, pc, pn, per_head(q_s), per_head(k_s),
                                cache_t(cache_k_win), cache_t(cache_v_win), per_head(v_s))

    window = lambda t: t.reshape(batch, N_HEADS, HEAD_DIM, CACHE_LEN).transpose(0, 3, 1, 2)[None]
    conv_a_prompt = glu_tail[:, CONV_HALO - (CONV_A_WIDTH - 1):][None]
    conv_ffn_prompt = st_p[:, FFN_HALO - 2:][None]

    o_s = o4.transpose(2, 0, 1, 3).reshape(rows, D_ATTN)
    past_f = state_conv_ffn[0].transpose(1, 0, 2).reshape(2 * nseq, D_FF)
    y_s, st_f = _ffn_sample(xs, a_s.reshape(rows, D_CONV), o_s, past_f, *ffn_w)

    y_sample = y_s.reshape(dec_seq, nseq, D_MODEL).transpose(1, 0, 2)
    new_rows = lambda t: t.reshape(dec_seq, nseq, N_HEADS, HEAD_DIM).transpose(1, 0, 2, 3)[None]
    conv_a_sample = st_a.transpose(1, 0, 2)[None]
    conv_ffn_sample = st_f.reshape(2, nseq, D_FF).transpose(1, 0, 2)[None]

    return (y_p, y_sample, window(kt_win), window(vt_win), conv_a_prompt, conv_ffn_prompt,
            new_rows(k_s), new_rows(v_s), conv_a_sample, conv_ffn_sample)
```

```python
import functools

import jax
import jax.numpy as jnp
from jax import lax
from jax.experimental import pallas as pl
from jax.experimental.pallas import tpu as pltpu

D_MODEL = 1024
D_CONV = 256
D_ATTN = 768
HEAD_DIM = 64
N_HEADS = 12
D_IN = 2 * D_CONV + 3 * D_ATTN
CONV_A_WIDTH = 31
D_FF = 2816
EPS = 1e-6
NEG_INF = -1e30
SCALE = HEAD_DIM ** -0.5
SCALE_LOG2E = SCALE * 1.4426950408889634

DILATIONS = (1, 4, 16)
NEIGHBOURS = 128
QBLOCK = 128
SPAN = DILATIONS[-1] * QBLOCK
CACHE_LEN = 2048
DEC_SEQ = 8

LANES = 128
FF_CHUNK = 256
N_FF_CHUNKS = D_FF // FF_CHUNK
ROW_TILE = 256
VMEM_LIMIT = 56 * 1024 * 1024

F32 = jnp.float32
BF16 = jnp.bfloat16


def _const_spec(shape):
    nd = len(shape)
    return pl.BlockSpec(shape, lambda *_: (0,) * nd, pipeline_mode=pl.Buffered(1))


def _params(n_axes):
    return pltpu.CompilerParams(
        dimension_semantics=("arbitrary",) * n_axes, vmem_limit_bytes=VMEM_LIMIT)


def _rms_norm(x, g):
    ms = jnp.mean(x * x, axis=-1, keepdims=True)
    return x * lax.rsqrt(ms + EPS) * g


def _dot(a, b):
    return jnp.dot(a, b, preferred_element_type=F32)


def _dot_nt(a, b):
    return lax.dot_general(a, b, (((1,), (1,)), ((), ())), preferred_element_type=F32)


K_COLS = (2 * D_CONV + D_ATTN, 2 * D_CONV + 2 * D_ATTN)
V_COLS = (2 * D_CONV + 2 * D_ATTN, D_IN)


def _normed(x_ref, g_ref):
    return _rms_norm(x_ref[...], g_ref[...]).astype(BF16)


def _project_glu(h, w_ref, glu_ref):
    a_val = _dot(h, w_ref[:, 0:D_CONV])
    a_gate = _dot(h, w_ref[:, D_CONV:2 * D_CONV])
    glu_ref[...] = a_val * jax.nn.sigmoid(a_gate)


def _project_qkv(h, w_ref, q_ref, k_ref, v_ref):
    q_ref[...] = _dot(h, w_ref[:, 2 * D_CONV:K_COLS[0]])
    k_ref[...] = _dot(h, w_ref[:, K_COLS[0]:K_COLS[1]])
    v_ref[...] = _dot(h, w_ref[:, V_COLS[0]:V_COLS[1]])


def _in_proj_sample_kernel(x_ref, g_ref, w_ref, glu_ref, q_ref, k_ref, v_ref):
    h = _normed(x_ref, g_ref)
    _project_glu(h, w_ref, glu_ref)
    _project_qkv(h, w_ref, q_ref, k_ref, v_ref)


def _in_proj_prompt_kernel(x_ref, g_ref, w_ref, wkt_ref, wvt_ref, cw_ref, cb_ref, lg_ref, lb_ref,
                           sq_ref, skn_ref, skc_ref,
                           a_ref, q_ref, k_ref, v_ref, kt_ref, vt_ref, tail_ref, spc_ref, spn_ref,
                           win_ref, sh_ref, *, first_window_tile):
    @pl.when(pl.program_id(1) == 0)
    def _():
        win_ref[0:CONV_HALO, :] = jnp.zeros((CONV_HALO, D_CONV), F32)

    @pl.when(pl.program_id(1) > 0)
    def _():
        win_ref[0:CONV_HALO, :] = win_ref[ROW_TILE:ROW_TILE + CONV_HALO, :]

    _sample_probs(sq_ref, skn_ref, skc_ref, spc_ref, spn_ref, first_head=0)
    h = _normed(x_ref, g_ref)
    _project_glu(h, w_ref, win_ref.at[CONV_HALO:CONV_HALO + ROW_TILE])
    tail_ref[...] = win_ref[ROW_TILE:ROW_TILE + CONV_HALO, :]

    first = CONV_HALO - (CONV_A_WIDTH - 1)
    for s in range(1, 8):
        sh_ref[s - 1] = win_ref[s:s + CONV_SHIFTED_ROWS, :]
    for c in range(ROW_TILE // CONV_ROWS):
        r0 = c * CONV_ROWS
        acc = jnp.broadcast_to(cb_ref[...], (CONV_ROWS, D_CONV))
        for j in range(CONV_A_WIDTH):
            whole, s = divmod(first + j, 8)
            src = win_ref if s == 0 else sh_ref.at[s - 1]
            acc = acc + cw_ref[j:j + 1, :] * src[r0 + 8 * whole:r0 + 8 * whole + CONV_ROWS, :]
        a_ref[r0:r0 + CONV_ROWS, :] = _layer_norm_silu(acc, lg_ref[...], lb_ref[...])

    _project_qkv(h, w_ref, q_ref, k_ref, v_ref)

    @pl.when(pl.program_id(1) >= first_window_tile)
    def _():
        kt_ref[...] = _dot_nt(wkt_ref[...], h)
        vt_ref[...] = _dot_nt(wvt_ref[...], h)


def _in_proj_out_shapes(lead):
    return ([jax.ShapeDtypeStruct(lead + (D_CONV,), F32)]
            + [jax.ShapeDtypeStruct(lead + (D_ATTN,), F32)] * 3)


def _in_proj_sample(x2d, g, w_b):
    rows = x2d.shape[0]
    row = lambda width: pl.BlockSpec((ROW_TILE, width), lambda i: (i, 0))
    return pl.pallas_call(
        _in_proj_sample_kernel,
        grid=(rows // ROW_TILE,),
        in_specs=[row(D_MODEL), _const_spec((1, D_MODEL)), _const_spec((D_MODEL, D_IN))],
        out_specs=[row(D_CONV), row(D_ATTN), row(D_ATTN), row(D_ATTN)],
        out_shape=_in_proj_out_shapes((rows,)),
        compiler_params=_params(1),
        name="in_proj_sample",
    )(x2d, g, w_b)


def _in_proj_prompt(x3d, g, w_b, wkt_b, wvt_b, cw, cb, lg, lb, q4, kn4, kc_t):
    batch, seq, _ = x3d.shape
    nseq = q4.shape[0]
    steps = seq // ROW_TILE
    first = (seq - CACHE_LEN) // ROW_TILE
    row = lambda width: pl.BlockSpec((None, ROW_TILE, width), lambda b, j: (b, j, 0))
    win = pl.BlockSpec((None, D_ATTN, ROW_TILE), lambda b, j: (b, 0, jnp.maximum(j - first, 0)))
    tail = pl.BlockSpec((None, CONV_HALO, D_CONV), lambda b, j: (b, 0, 0))
    new = _per_sequence_spec(SAMPLE_NEW_SHAPE, steps)
    return pl.pallas_call(
        functools.partial(_in_proj_prompt_kernel, first_window_tile=first),
        grid=(batch, steps),
        in_specs=[row(D_MODEL), _const_spec((1, D_MODEL)), _const_spec((D_MODEL, D_IN)),
                  _const_spec((D_ATTN, D_MODEL)), _const_spec((D_ATTN, D_MODEL)),
                  _const_spec((CONV_A_WIDTH, D_CONV))] + [_const_spec((1, D_CONV))] * 3
        + [new, new, _per_sequence_spec(SAMPLE_HALF_CACHE_SHAPE, steps)],
        out_specs=[row(D_CONV), row(D_ATTN), row(D_ATTN), row(D_ATTN), win, win, tail,
                   _per_sequence_spec(SAMPLE_PC_SHAPE, steps),
                   _per_sequence_spec(SAMPLE_PN_SHAPE, steps)],
        out_shape=_in_proj_out_shapes((batch, seq))
        + [jax.ShapeDtypeStruct((batch, D_ATTN, CACHE_LEN), F32)] * 2
        + [jax.ShapeDtypeStruct((batch, CONV_HALO, D_CONV), F32),
           jax.ShapeDtypeStruct((nseq,) + SAMPLE_PC_SHAPE, BF16),
           jax.ShapeDtypeStruct((nseq,) + SAMPLE_PN_SHAPE, F32)],
        scratch_shapes=[pltpu.VMEM((CONV_HALO + ROW_TILE, D_CONV), F32),
                        pltpu.VMEM((7, CONV_SHIFTED_ROWS, D_CONV), F32)],
        compiler_params=_params(2),
        name="in_proj_prompt",
    )(x3d, g, w_b, wkt_b, wvt_b, cw, cb, lg, lb, q4, kn4, kc_t)


def _layer_norm_silu(y, g, b):
    mu = jnp.mean(y, axis=-1, keepdims=True)
    yc = y - mu
    z = yc * lax.rsqrt(jnp.mean(yc * yc, axis=-1, keepdims=True) + EPS) * g + b
    return z * jax.nn.sigmoid(z)


CONV_HALO = 32
CONV_ROWS = 64
CONV_SHIFTED_ROWS = ROW_TILE + CONV_HALO - 8


def _conv_a_sample_kernel(past_ref, glu_ref, w_ref, b_ref, lg_ref, lb_ref, a_ref, state_ref):
    n_past = CONV_A_WIDTH - 1

    def row(u):
        return past_ref[u] if u < n_past else glu_ref[u - n_past]

    for t in range(DEC_SEQ):
        acc = jnp.broadcast_to(b_ref[...], a_ref.shape[1:])
        for j in range(CONV_A_WIDTH):
            acc = acc + w_ref[j:j + 1, :] * row(t + j)
        a_ref[t] = _layer_norm_silu(acc, lg_ref[...], lb_ref[...])
    for u in range(n_past):
        state_ref[u] = row(u + DEC_SEQ)


def _conv_a_sample(past_t, glu_t, w, b, lg, lb):
    n_past, nseq, _ = past_t.shape
    return pl.pallas_call(
        _conv_a_sample_kernel,
        out_shape=[jax.ShapeDtypeStruct((DEC_SEQ, nseq, D_CONV), F32),
                   jax.ShapeDtypeStruct((n_past, nseq, D_CONV), F32)],
        compiler_params=pltpu.CompilerParams(vmem_limit_bytes=VMEM_LIMIT),
        name="conv_a_sample",
    )(past_t, glu_t, w, b, lg, lb)


def _group_geometry(d):
    nb = SPAN // (d * QBLOCK)
    slot = (nb + 1) * QBLOCK
    return nb, slot


def _attn_prompt_kernel(q_ref, k_ref, v_ref, o_ref, *scratch):
    n_g = len(DILATIONS)
    qg = scratch[0:n_g]
    kg = scratch[n_g:2 * n_g]
    vg = scratch[2 * n_g:3 * n_g]
    acc_n = scratch[3 * n_g:4 * n_g]
    den_n = scratch[4 * n_g:5 * n_g]
    max_n = scratch[5 * n_g:6 * n_g]
    stage_ref, s_ref, p_ref, bias_ref = scratch[6 * n_g:]
    sp = pl.program_id(2)
    lane = lax.broadcasted_iota(jnp.int32, (QBLOCK, LANES), 1)
    head0 = lane < HEAD_DIM

    @pl.when(sp == 0)
    def _():
        for g, d in enumerate(DILATIONS):
            _, slot = _group_geometry(d)
            vg[g][:, LANES:] = jnp.ones((d * slot, LANES), BF16)
            for r in range(d):
                kg[g][r * slot:r * slot + QBLOCK, :] = jnp.zeros((QBLOCK, LANES), BF16)
                vg[g][r * slot:r * slot + QBLOCK, 0:LANES] = jnp.zeros((QBLOCK, LANES), BF16)

    @pl.when(sp > 0)
    def _():
        for g, d in enumerate(DILATIONS):
            nb, slot = _group_geometry(d)
            for r in range(d):
                lo = r * slot
                kg[g][lo:lo + QBLOCK, :] = kg[g][lo + nb * QBLOCK:lo + slot, :]
                vg[g][lo:lo + QBLOCK, :] = vg[g][lo + nb * QBLOCK:lo + slot, :]

    def put(g, r, n, rows_q, x_q, x_k, x_v):
        _, slot = _group_geometry(DILATIONS[g])
        qg[g][rows_q, :] = (x_q * SCALE_LOG2E).astype(BF16)
        lo = r * slot + QBLOCK
        kg[g][lo:lo + n, :] = x_k.astype(BF16)
        vg[g][lo:lo + n, 0:LANES] = x_v.astype(BF16)

    put(0, 0, SPAN, slice(0, SPAN), q_ref[...], k_ref[...], v_ref[...])
    d1, d2 = DILATIONS[1], DILATIONS[2] // DILATIONS[1]
    n1 = SPAN // d1
    for r in range(d1):
        xs = [ref[pl.ds(r, n1, stride=d1), :] for ref in (q_ref, k_ref, v_ref)]
        for t, x in enumerate(xs):
            stage_ref[t, r * n1:(r + 1) * n1, :] = x
        put(1, r, n1, slice(r * n1, (r + 1) * n1), *xs)
    n2 = n1 // d2
    for r in range(DILATIONS[2]):
        rows = pl.ds((r % d1) * n1 + r // d1, n2, stride=d2)
        xs = [stage_ref[t, rows, :] for t in range(3)]
        put(2, r, n2, slice(r * n2, (r + 1) * n2), *xs)

    qi = lax.broadcasted_iota(jnp.int32, (QBLOCK, 2 * QBLOCK), 0)
    kc = lax.broadcasted_iota(jnp.int32, (QBLOCK, 2 * QBLOCK), 1)
    band = (kc >= qi) & (kc <= qi + NEIGHBOURS)
    bias_ref[0] = jnp.where(band, 0.0, NEG_INF)
    bias_ref[1] = jnp.where(band & (kc >= QBLOCK), 0.0, NEG_INF)

    def key_rows(u, nb):
        return pl.ds((u + u // nb) * QBLOCK, 2 * QBLOCK)

    def scores(u):
        zero = jnp.zeros((QBLOCK, LANES), BF16)
        for g, d in enumerate(DILATIONS):
            nb, _ = _group_geometry(d)
            kb = kg[g][key_rows(u, nb), :]
            qb = qg[g][pl.ds(u * QBLOCK, QBLOCK), :]
            s_ref[u % 2, 2 * g] = _dot_nt(jnp.where(head0, qb, zero), kb)
            s_ref[u % 2, 2 * g + 1] = _dot_nt(jnp.where(head0, zero, qb), kb)

    def result_rows(u, g):
        if g == 2:
            return pl.ds((u % d1) * n1 + u // d1, QBLOCK, stride=d2)
        return pl.ds(u * QBLOCK, QBLOCK)

    def softmax(u):
        for g, d in enumerate(DILATIONS):
            nb, _ = _group_geometry(d)
            first = jnp.where(sp == 0, 1, 0) if u % nb == 0 else 0
            ms = []
            for h in range(2):
                s = s_ref[u % 2, 2 * g + h] + bias_ref[first]
                m = jnp.max(s, axis=-1, keepdims=True)
                p_ref[u % 2, 2 * g + h] = jnp.exp2(s - m).astype(BF16)
                ms.append(m)
            max_n[g][result_rows(u, g), :] = jnp.where(head0, ms[0], ms[1])

    def values(u):
        for g, d in enumerate(DILATIONS):
            nb, _ = _group_geometry(d)
            vb = vg[g][key_rows(u, nb), :]
            e0 = _dot(p_ref[u % 2, 2 * g], vb)
            e1 = _dot(p_ref[u % 2, 2 * g + 1], vb)
            rows = result_rows(u, g)
            acc_n[g][rows, :] = jnp.where(head0, e0[:, :LANES], e1[:, :LANES])
            den_n[g][rows, :] = jnp.where(head0, e0[:, LANES:], e1[:, LANES:])

    n_units = SPAN // QBLOCK
    for u in range(n_units + 2):
        if u >= 2:
            values(u - 2)
        if 1 <= u <= n_units:
            softmax(u - 1)
        if u < n_units:
            scores(u)

    def merge(c, carry):
        major = pl.ds(pl.multiple_of(c * QBLOCK, QBLOCK), QBLOCK)
        by_pos = pl.ds(lax.bitwise_and(c, d1 - 1) * (QBLOCK * d1) + lax.shift_right_logical(c, 2),
                       QBLOCK, stride=d1)
        rows = (by_pos, major, major)
        ms = [max_n[g][rows[g], :] for g in range(n_g)]
        m_all = functools.reduce(jnp.maximum, ms)
        ws = [jnp.exp2(m - m_all) for m in ms]
        num = sum(w * acc_n[g][rows[g], :] for g, w in enumerate(ws))
        den = sum(w * den_n[g][rows[g], :] for g, w in enumerate(ws))
        o_ref[by_pos, :] = num / den
        return carry

    lax.fori_loop(0, n_units, merge, 0)


def _attn_prompt(q, k, v):
    batch, seq, _ = q.shape
    blk = pl.BlockSpec((None, SPAN, LANES), lambda b, hp, s: (b, s, hp))
    n_g = len(DILATIONS)
    scratch = [pltpu.VMEM((SPAN, LANES), BF16) for _ in DILATIONS]
    scratch += [pltpu.VMEM((d * _group_geometry(d)[1], LANES), BF16) for d in DILATIONS]
    scratch += [pltpu.VMEM((d * _group_geometry(d)[1], 2 * LANES), BF16) for d in DILATIONS]
    scratch += [pltpu.VMEM((SPAN, LANES), F32) for _ in range(3 * n_g)]
    scratch += [pltpu.VMEM((3, SPAN, LANES), F32),
                pltpu.VMEM((2, 2 * n_g, QBLOCK, 2 * QBLOCK), F32),
                pltpu.VMEM((2, 2 * n_g, QBLOCK, 2 * QBLOCK), BF16),
                pltpu.VMEM((2, QBLOCK, 2 * QBLOCK), F32)]
    return pl.pallas_call(
        _attn_prompt_kernel,
        grid=(batch, D_ATTN // LANES, seq // SPAN),
        in_specs=[blk, blk, blk],
        out_specs=blk,
        out_shape=jax.ShapeDtypeStruct((batch, seq, D_ATTN), F32),
        scratch_shapes=scratch,
        compiler_params=_params(3),
        name="attn_prompt",
    )(q, k, v)


def _group_count(dist):
    total = None
    for d in DILATIONS:
        sh = d.bit_length() - 1
        hit = ((dist & (d - 1)) == 0) & ((dist >> sh) <= NEIGHBOURS) & (dist >= 0)
        total = hit.astype(F32) if total is None else total + hit.astype(F32)
    return total


EARLY_HEADS = N_HEADS // 2
SAMPLE_NEW_SHAPE = (N_HEADS, DEC_SEQ, HEAD_DIM)
SAMPLE_CACHE_SHAPE = (N_HEADS, HEAD_DIM, CACHE_LEN)
SAMPLE_HALF_CACHE_SHAPE = (EARLY_HEADS, HEAD_DIM, CACHE_LEN)
SAMPLE_PC_SHAPE = (EARLY_HEADS, DEC_SEQ, CACHE_LEN)
SAMPLE_PN_SHAPE = (EARLY_HEADS, DEC_SEQ, DEC_SEQ)


def _sample_probs(q_ref, kn_ref, kc_ref, pc_ref, pn_ref, first_head):
    qi = lax.broadcasted_iota(jnp.int32, (DEC_SEQ, CACHE_LEN), 0)
    pos = lax.broadcasted_iota(jnp.int32, (DEC_SEQ, CACHE_LEN), 1)
    cnt_c = _group_count(CACHE_LEN + qi - pos)
    qn = lax.broadcasted_iota(jnp.int32, (DEC_SEQ, DEC_SEQ), 0)
    kn = lax.broadcasted_iota(jnp.int32, (DEC_SEQ, DEC_SEQ), 1)
    cnt_n = _group_count(qn - kn)
    for i in range(kc_ref.shape[0]):
        h = first_head + i
        qh = (q_ref[h] * SCALE_LOG2E).astype(BF16)
        s_c = jnp.where(cnt_c > 0, _dot(qh, kc_ref[i].astype(BF16)), NEG_INF)
        s_n = jnp.where(cnt_n > 0, _dot_nt(qh, kn_ref[h].astype(BF16)), NEG_INF)
        m = jnp.maximum(jnp.max(s_c, axis=-1, keepdims=True), jnp.max(s_n, axis=-1, keepdims=True))
        w_c = cnt_c * jnp.exp2(s_c - m)
        w_n = cnt_n * jnp.exp2(s_n - m)
        den = jnp.sum(w_c, axis=-1, keepdims=True) + jnp.sum(w_n, axis=-1, keepdims=True)
        pc_ref[i] = (w_c / den).astype(BF16)
        pn_ref[i] = w_n / den


def _sample_values(pc_ref, pn_ref, vc_ref, vn_ref, o_ref, first_head):
    for i in range(pc_ref.shape[0]):
        h = first_head + i
        o_ref[h] = (_dot_nt(pc_ref[i], vc_ref[h].astype(BF16))
                    + _dot(pn_ref[i].astype(BF16), vn_ref[h].astype(BF16)))


def _per_sequence_spec(shape, steps_per_batch, head_block=0):
    rest = (head_block,) + (0,) * (len(shape) - 1)
    return pl.BlockSpec((None,) + shape, lambda b, j: (b * steps_per_batch + j,) + rest)


FFN_HALO = 8


def _mix_ffn(x_ref, a_ref, o_ref, wo_ref, g2_ref, wg_ref, wu_ref, cw_ref, wd_ref, gf_ref, y_ref,
             x1_ref, h_ref, act_ref, conv):
    x1 = x_ref[...] + _dot(a_ref[...].astype(BF16), wo_ref[0:D_CONV, :])
    x1 = x1 + _dot(o_ref[...].astype(BF16), wo_ref[D_CONV:, :])
    x1_ref[...] = x1
    h_ref[...] = _rms_norm(x1, g2_ref[...]).astype(BF16)
    for c in range(N_FF_CHUNKS):
        cols = slice(c * FF_CHUNK, (c + 1) * FF_CHUNK)
        h = h_ref[...]
        g = conv(c, cols, _dot(h, wg_ref[:, cols]), cw_ref[:, cols])
        act_ref[:, cols] = ((g * jax.nn.sigmoid(g)) * _dot(h, wu_ref[:, cols])).astype(BF16)
    y_ref[...] = _rms_norm(x1_ref[...] + _dot(act_ref[...], wd_ref[...]), gf_ref[...])


def _conv_taps(cw, now, back1, back2):
    return cw[2:3, :] * now + cw[1:2, :] * back1 + cw[0:1, :] * back2 + cw[3:4, :]


def _ffn_prompt_kernel(x_ref, a_ref, o_ref, wo_ref, g2_ref, wg_ref, wu_ref, cw_ref, wd_ref, gf_ref,
                       spc_ref, spn_ref, sq_ref, skn_ref, skc_ref, svc_ref, svn_ref,
                       y_ref, st_ref, so_ref, x1_ref, h_ref, act_ref, gbuf_ref, carry_ref,
                       spc_late_ref, spn_late_ref):
    @pl.when(pl.program_id(1) == 0)
    def _():
        carry_ref[...] = jnp.zeros(carry_ref.shape, F32)

    def conv(c, cols, gp, cw):
        buf = gbuf_ref.at[c % 2]
        buf[0:FFN_HALO, :] = carry_ref[:, cols]
        buf[FFN_HALO:, :] = gp
        tail = gp[ROW_TILE - FFN_HALO:, :]
        carry_ref[:, cols] = tail
        st_ref[:, cols] = tail
        return _conv_taps(cw, gp, buf[FFN_HALO - 1:FFN_HALO - 1 + ROW_TILE, :],
                          buf[FFN_HALO - 2:FFN_HALO - 2 + ROW_TILE, :])

    _sample_probs(sq_ref, skn_ref, skc_ref, spc_late_ref, spn_late_ref, first_head=EARLY_HEADS)
    _mix_ffn(x_ref, a_ref, o_ref, wo_ref, g2_ref, wg_ref, wu_ref, cw_ref, wd_ref, gf_ref, y_ref,
             x1_ref, h_ref, act_ref, conv)
    _sample_values(spc_ref, spn_ref, svc_ref, svn_ref, so_ref, first_head=0)
    _sample_values(spc_late_ref, spn_late_ref, svc_ref, svn_ref, so_ref, first_head=EARLY_HEADS)


def _ffn_weight_specs():
    return [_const_spec((D_MODEL, D_MODEL)), _const_spec((1, D_MODEL)),
            _const_spec((D_MODEL, D_FF)), _const_spec((D_MODEL, D_FF)),
            _const_spec((8, D_FF)), _const_spec((D_FF, D_MODEL)), _const_spec((1, D_MODEL))]


def _ffn_scratch(rows):
    return [pltpu.VMEM((rows, D_MODEL), F32), pltpu.VMEM((rows, D_MODEL), BF16),
            pltpu.VMEM((rows, D_FF), BF16)]


def _ffn_prompt(x, a, o, wo, g2, wg, wu, cw, wd, gf, pc, pn, q4, kn4, kc_t, vc_t, vn4):
    batch, seq, _ = x.shape
    nseq = pc.shape[0]
    steps = seq // ROW_TILE
    row = lambda width: pl.BlockSpec((None, ROW_TILE, width), lambda b, j: (b, j, 0))
    st = pl.BlockSpec((None, FFN_HALO, D_FF), lambda b, j: (b, 0, 0))
    new = _per_sequence_spec(SAMPLE_NEW_SHAPE, steps)
    return pl.pallas_call(
        _ffn_prompt_kernel,
        grid=(batch, steps),
        in_specs=[row(D_MODEL), row(D_CONV), row(D_ATTN)] + _ffn_weight_specs()
        + [_per_sequence_spec(SAMPLE_PC_SHAPE, steps),
           _per_sequence_spec(SAMPLE_PN_SHAPE, steps), new, new,
           _per_sequence_spec(SAMPLE_HALF_CACHE_SHAPE, steps, head_block=1),
           _per_sequence_spec(SAMPLE_CACHE_SHAPE, steps), new],
        out_specs=[row(D_MODEL), st, new],
        out_shape=[jax.ShapeDtypeStruct((batch, seq, D_MODEL), F32),
                   jax.ShapeDtypeStruct((batch, FFN_HALO, D_FF), F32),
                   jax.ShapeDtypeStruct((nseq,) + SAMPLE_NEW_SHAPE, F32)],
        scratch_shapes=_ffn_scratch(ROW_TILE)
        + [pltpu.VMEM((2, FFN_HALO + ROW_TILE, FF_CHUNK), F32), pltpu.VMEM((FFN_HALO, D_FF), F32),
           pltpu.VMEM(SAMPLE_PC_SHAPE, BF16), pltpu.VMEM(SAMPLE_PN_SHAPE, F32)],
        compiler_params=_params(2),
        name="ffn_prompt",
    )(x, a, o, wo, g2, wg, wu, cw, wd, gf, pc, pn, q4, kn4, kc_t, vc_t, vn4)


def _ffn_sample_kernel(x_ref, a_ref, o_ref, past_ref, wo_ref, g2_ref, wg_ref, wu_ref, cw_ref, wd_ref,
                       gf_ref, y_ref, st_ref, x1_ref, h_ref, act_ref, gbuf_ref):
    rows = x_ref.shape[0]
    nseq = rows // DEC_SEQ

    def conv(c, cols, gp, cw):
        buf = gbuf_ref.at[c % 2]
        buf[0:2 * nseq, :] = past_ref[:, cols]
        buf[2 * nseq:, :] = gp
        st_ref[:, cols] = gp[rows - 2 * nseq:, :]
        return _conv_taps(cw, gp, buf[nseq:nseq + rows, :], buf[0:rows, :])

    _mix_ffn(x_ref, a_ref, o_ref, wo_ref, g2_ref, wg_ref, wu_ref, cw_ref, wd_ref, gf_ref, y_ref,
             x1_ref, h_ref, act_ref, conv)


def _ffn_sample(x, a, o, past, wo, g2, wg, wu, cw, wd, gf):
    rows = x.shape[0]
    nseq = rows // DEC_SEQ
    return pl.pallas_call(
        _ffn_sample_kernel,
        out_shape=[jax.ShapeDtypeStruct((rows, D_MODEL), F32),
                   jax.ShapeDtypeStruct((2 * nseq, D_FF), F32)],
        scratch_shapes=_ffn_scratch(rows) + [pltpu.VMEM((2, 2 * nseq + rows, FF_CHUNK), F32)],
        compiler_params=pltpu.CompilerParams(vmem_limit_bytes=VMEM_LIMIT),
        name="ffn_sample",
    )(x, a, o, past, wo, g2, wg, wu, cw, wd, gf)


def kernel(x_prompt, x_sample, cache_k_win, cache_v_win, state_conv_a, state_conv_ffn, norm_mix_g, w_in, conv_a_w, conv_a_b, ln_a_g, ln_a_b, w_out, norm_ffn_g, w_ffn_gate, w_ffn_up, ffn_conv_w, ffn_conv_b, w_ffn_down, norm_final_g):
    batch, seq, _ = x_prompt.shape
    nseq, dec_seq, _ = x_sample.shape
    assert w_in.shape[0] == 1, "single trunk layer"
    assert dec_seq == DEC_SEQ and cache_k_win.shape[2] == CACHE_LEN
    assert seq % SPAN == 0 and seq >= CACHE_LEN and (nseq * dec_seq) % ROW_TILE == 0
    assert batch * (seq // ROW_TILE) == nseq, "one sample sequence per prompt row tile"

    g1 = norm_mix_g
    w_in_b = w_in[0].astype(BF16)
    wkt_b = w_in_b[:, K_COLS[0]:K_COLS[1]].T
    wvt_b = w_in_b[:, V_COLS[0]:V_COLS[1]].T
    ca = (conv_a_w[0], conv_a_b, ln_a_g, ln_a_b)
    cw = jnp.concatenate([ffn_conv_w[0], ffn_conv_b, jnp.zeros((4, D_FF), F32)], axis=0)
    ffn_w = (w_out[0].astype(BF16), norm_ffn_g, w_ffn_gate[0].astype(BF16), w_ffn_up[0].astype(BF16),
             cw, w_ffn_down[0].astype(BF16), norm_final_g[None, :])

    rows = dec_seq * nseq
    xs = x_sample.transpose(1, 0, 2).reshape(rows, D_MODEL)
    glu_s, q_s, k_s, v_s = _in_proj_sample(xs, g1, w_in_b)
    a_s, st_a = _conv_a_sample(state_conv_a[0].transpose(1, 0, 2),
                               glu_s.reshape(dec_seq, nseq, D_CONV), *ca)
    per_head = lambda t: t.reshape(dec_seq, nseq, N_HEADS, HEAD_DIM).transpose(1, 2, 0, 3)
    cache_t = lambda t: t[0].transpose(0, 2, 3, 1)

    a_p, q_p, k_p, v_p, kt_win, vt_win, glu_tail, pc, pn = _in_proj_prompt(
        x_prompt, g1, w_in_b, wkt_b, wvt_b, *ca, per_head(q_s), per_head(k_s), cache_t(cache_k_win))
    o_p = _attn_prompt(q_p, k_p, v_p)
    y_p, st_p, o4 = _ffn_prompt(x_prompt, a_p, o_p, *ffn_w, pc, pn, per_head(q_s), per_head(k_s),
                                cache_t(cache_k_win), cache_t(cache_v_win), per_head(v_s))

    window = lambda t: t.reshape(batch, N_HEADS, HEAD_DIM, CACHE_LEN).transpose(0, 3, 1, 2)[None]
    conv_a_prompt = glu_tail[:, CONV_HALO - (CONV_A_WIDTH - 1):][None]
    conv_ffn_prompt = st_p[:, FFN_HALO - 2:][None]

    o_s = o4.transpose(2, 0, 1, 3).reshape(rows, D_ATTN)
    past_f = state_conv_ffn[0].transpose(1, 0, 2).reshape(2 * nseq, D_FF)
    y_s, st_f = _ffn_sample(xs, a_s.reshape(rows, D_CONV), o_s, past_f, *ffn_w)

    y_sample = y_s.reshape(dec_seq, nseq, D_MODEL).transpose(1, 0, 2)
    new_rows = lambda t: t.reshape(dec_seq, nseq, N_HEADS, HEAD_DIM).transpose(1, 0, 2, 3)[None]
    conv_a_sample = st_a.transpose(1, 0, 2)[None]
    conv_ffn_sample = st_f.reshape(2, nseq, D_FF).transpose(1, 0, 2)[None]

    return (y_p, y_sample, window(kt_win), window(vt_win), conv_a_prompt, conv_ffn_prompt,
            new_rows(k_s), new_rows(v_s), conv_a_sample, conv_ffn_sample)
```

```python
import functools

import jax
import jax.numpy as jnp
from jax import lax
from jax.experimental import pallas as pl
from jax.experimental.pallas import tpu as pltpu

D_MODEL = 1024
D_CONV = 256
D_ATTN = 768
HEAD_DIM = 64
N_HEADS = 12
D_IN = 2 * D_CONV + 3 * D_ATTN
CONV_A_WIDTH = 31
D_FF = 2816
EPS = 1e-6
NEG_INF = -1e30
SCALE = HEAD_DIM ** -0.5
SCALE_LOG2E = SCALE * 1.4426950408889634

DILATIONS = (1, 4, 16)
NEIGHBOURS = 128
QBLOCK = 128
SPAN = DILATIONS[-1] * QBLOCK
CACHE_LEN = 2048
DEC_SEQ = 8

LANES = 128
FF_CHUNK = 256
N_FF_CHUNKS = D_FF // FF_CHUNK
ROW_TILE = 256
VMEM_LIMIT = 60 * 1024 * 1024

F32 = jnp.float32
BF16 = jnp.bfloat16


def _const_spec(shape):
    nd = len(shape)
    return pl.BlockSpec(shape, lambda *_: (0,) * nd, pipeline_mode=pl.Buffered(1))


def _params(n_axes):
    return pltpu.CompilerParams(
        dimension_semantics=("arbitrary",) * n_axes, vmem_limit_bytes=VMEM_LIMIT)


def _rms_norm(x, g):
    ms = jnp.mean(x * x, axis=-1, keepdims=True)
    return x * lax.rsqrt(ms + EPS) * g


def _dot(a, b):
    return jnp.dot(a, b, preferred_element_type=F32)


def _dot_nt(a, b):
    return lax.dot_general(a, b, (((1,), (1,)), ((), ())), preferred_element_type=F32)


K_COLS = (2 * D_CONV + D_ATTN, 2 * D_CONV + 2 * D_ATTN)
V_COLS = (2 * D_CONV + 2 * D_ATTN, D_IN)


def _normed(x_ref, g_ref):
    return _rms_norm(x_ref[...], g_ref[...]).astype(BF16)


def _project_glu(h, w_ref, glu_ref):
    a_val = _dot(h, w_ref[:, 0:D_CONV])
    a_gate = _dot(h, w_ref[:, D_CONV:2 * D_CONV])
    glu_ref[...] = a_val * jax.nn.sigmoid(a_gate)


def _project_qkv(h, w_ref, q_ref, k_ref, v_ref):
    q_ref[...] = _dot(h, w_ref[:, 2 * D_CONV:K_COLS[0]])
    k_ref[...] = _dot(h, w_ref[:, K_COLS[0]:K_COLS[1]])
    v_ref[...] = _dot(h, w_ref[:, V_COLS[0]:V_COLS[1]])


def _in_proj_sample_kernel(x_ref, g_ref, w_ref, glu_ref, q_ref, k_ref, v_ref):
    h = _normed(x_ref, g_ref)
    _project_glu(h, w_ref, glu_ref)
    _project_qkv(h, w_ref, q_ref, k_ref, v_ref)


def _in_proj_prompt_kernel(x_ref, g_ref, w_ref, wkt_ref, wvt_ref, cw_ref, cb_ref, lg_ref, lb_ref,
                           a_ref, q_ref, k_ref, v_ref, kt_ref, vt_ref, tail_ref,
                           win_ref, sh_ref, *, first_window_tile):
    @pl.when(pl.program_id(1) == 0)
    def _():
        win_ref[0:CONV_HALO, :] = jnp.zeros((CONV_HALO, D_CONV), F32)

    @pl.when(pl.program_id(1) > 0)
    def _():
        win_ref[0:CONV_HALO, :] = win_ref[ROW_TILE:ROW_TILE + CONV_HALO, :]

    h = _normed(x_ref, g_ref)
    _project_glu(h, w_ref, win_ref.at[CONV_HALO:CONV_HALO + ROW_TILE])
    tail_ref[...] = win_ref[ROW_TILE:ROW_TILE + CONV_HALO, :]

    first = CONV_HALO - (CONV_A_WIDTH - 1)
    for s in range(1, 8):
        sh_ref[s - 1] = win_ref[s:s + CONV_SHIFTED_ROWS, :]
    for c in range(ROW_TILE // CONV_ROWS):
        r0 = c * CONV_ROWS
        acc = jnp.broadcast_to(cb_ref[...], (CONV_ROWS, D_CONV))
        for j in range(CONV_A_WIDTH):
            whole, s = divmod(first + j, 8)
            src = win_ref if s == 0 else sh_ref.at[s - 1]
            acc = acc + cw_ref[j:j + 1, :] * src[r0 + 8 * whole:r0 + 8 * whole + CONV_ROWS, :]
        a_ref[r0:r0 + CONV_ROWS, :] = _layer_norm_silu(acc, lg_ref[...], lb_ref[...])

    _project_qkv(h, w_ref, q_ref, k_ref, v_ref)

    @pl.when(pl.program_id(1) >= first_window_tile)
    def _():
        kt_ref[...] = _dot_nt(wkt_ref[...], h)
        vt_ref[...] = _dot_nt(wvt_ref[...], h)


def _in_proj_out_shapes(lead):
    return ([jax.ShapeDtypeStruct(lead + (D_CONV,), F32)]
            + [jax.ShapeDtypeStruct(lead + (D_ATTN,), F32)] * 3)


def _in_proj_sample(x2d, g, w_b):
    rows = x2d.shape[0]
    row = lambda width: pl.BlockSpec((ROW_TILE, width), lambda i: (i, 0))
    return pl.pallas_call(
        _in_proj_sample_kernel,
        grid=(rows // ROW_TILE,),
        in_specs=[row(D_MODEL), _const_spec((1, D_MODEL)), _const_spec((D_MODEL, D_IN))],
        out_specs=[row(D_CONV), row(D_ATTN), row(D_ATTN), row(D_ATTN)],
        out_shape=_in_proj_out_shapes((rows,)),
        compiler_params=_params(1),
        name="in_proj_sample",
    )(x2d, g, w_b)


def _in_proj_prompt(x3d, g, w_b, wkt_b, wvt_b, cw, cb, lg, lb):
    batch, seq, _ = x3d.shape
    steps = seq // ROW_TILE
    first = (seq - CACHE_LEN) // ROW_TILE
    row = lambda width: pl.BlockSpec((None, ROW_TILE, width), lambda b, j: (b, j, 0))
    win = pl.BlockSpec((None, D_ATTN, ROW_TILE), lambda b, j: (b, 0, jnp.maximum(j - first, 0)))
    tail = pl.BlockSpec((None, CONV_HALO, D_CONV), lambda b, j: (b, 0, 0))
    return pl.pallas_call(
        functools.partial(_in_proj_prompt_kernel, first_window_tile=first),
        grid=(batch, steps),
        in_specs=[row(D_MODEL), _const_spec((1, D_MODEL)), _const_spec((D_MODEL, D_IN)),
                  _const_spec((D_ATTN, D_MODEL)), _const_spec((D_ATTN, D_MODEL)),
                  _const_spec((CONV_A_WIDTH, D_CONV))] + [_const_spec((1, D_CONV))] * 3,
        out_specs=[row(D_CONV), row(D_ATTN), row(D_ATTN), row(D_ATTN), win, win, tail],
        out_shape=_in_proj_out_shapes((batch, seq))
        + [jax.ShapeDtypeStruct((batch, D_ATTN, CACHE_LEN), F32)] * 2
        + [jax.ShapeDtypeStruct((batch, CONV_HALO, D_CONV), F32)],
        scratch_shapes=[pltpu.VMEM((CONV_HALO + ROW_TILE, D_CONV), F32),
                        pltpu.VMEM((7, CONV_SHIFTED_ROWS, D_CONV), F32)],
        compiler_params=_params(2),
        name="in_proj_prompt",
    )(x3d, g, w_b, wkt_b, wvt_b, cw, cb, lg, lb)


def _layer_norm_silu(y, g, b):
    mu = jnp.mean(y, axis=-1, keepdims=True)
    yc = y - mu
    z = yc * lax.rsqrt(jnp.mean(yc * yc, axis=-1, keepdims=True) + EPS) * g + b
    return z * jax.nn.sigmoid(z)


CONV_HALO = 32
CONV_ROWS = 64
CONV_SHIFTED_ROWS = ROW_TILE + CONV_HALO - 8


def _conv_a_sample_kernel(past_ref, glu_ref, w_ref, b_ref, lg_ref, lb_ref, a_ref, state_ref):
    n_past = CONV_A_WIDTH - 1

    def row(u):
        return past_ref[u] if u < n_past else glu_ref[u - n_past]

    for t in range(DEC_SEQ):
        acc = jnp.broadcast_to(b_ref[...], a_ref.shape[1:])
        for j in range(CONV_A_WIDTH):
            acc = acc + w_ref[j:j + 1, :] * row(t + j)
        a_ref[t] = _layer_norm_silu(acc, lg_ref[...], lb_ref[...])
    for u in range(n_past):
        state_ref[u] = row(u + DEC_SEQ)


def _conv_a_sample(past_t, glu_t, w, b, lg, lb):
    n_past, nseq, _ = past_t.shape
    return pl.pallas_call(
        _conv_a_sample_kernel,
        out_shape=[jax.ShapeDtypeStruct((DEC_SEQ, nseq, D_CONV), F32),
                   jax.ShapeDtypeStruct((n_past, nseq, D_CONV), F32)],
        compiler_params=pltpu.CompilerParams(vmem_limit_bytes=VMEM_LIMIT),
        name="conv_a_sample",
    )(past_t, glu_t, w, b, lg, lb)


def _group_geometry(d):
    nb = SPAN // (d * QBLOCK)
    slot = (nb + 1) * QBLOCK
    return nb, slot


def _attn_prompt_kernel(q_ref, k_ref, v_ref, o_ref, *scratch):
    n_g = len(DILATIONS)
    qg = scratch[0:n_g]
    kg = scratch[n_g:2 * n_g]
    vg = scratch[2 * n_g:3 * n_g]
    acc_n = scratch[3 * n_g:4 * n_g]
    den_n = scratch[4 * n_g:5 * n_g]
    max_n = scratch[5 * n_g:6 * n_g]
    stage_ref, s_ref, p_ref, bias_ref = scratch[6 * n_g:]
    sp = pl.program_id(2)
    lane = lax.broadcasted_iota(jnp.int32, (QBLOCK, LANES), 1)
    head0 = lane < HEAD_DIM

    @pl.when(sp == 0)
    def _():
        for g, d in enumerate(DILATIONS):
            _, slot = _group_geometry(d)
            vg[g][:, LANES:] = jnp.ones((d * slot, LANES), BF16)
            for r in range(d):
                kg[g][r * slot:r * slot + QBLOCK, :] = jnp.zeros((QBLOCK, LANES), BF16)
                vg[g][r * slot:r * slot + QBLOCK, 0:LANES] = jnp.zeros((QBLOCK, LANES), BF16)

    @pl.when(sp > 0)
    def _():
        for g, d in enumerate(DILATIONS):
            nb, slot = _group_geometry(d)
            for r in range(d):
                lo = r * slot
                kg[g][lo:lo + QBLOCK, :] = kg[g][lo + nb * QBLOCK:lo + slot, :]
                vg[g][lo:lo + QBLOCK, :] = vg[g][lo + nb * QBLOCK:lo + slot, :]

    def put(g, r, n, rows_q, x_q, x_k, x_v):
        _, slot = _group_geometry(DILATIONS[g])
        qg[g][rows_q, :] = (x_q * SCALE_LOG2E).astype(BF16)
        lo = r * slot + QBLOCK
        kg[g][lo:lo + n, :] = x_k.astype(BF16)
        vg[g][lo:lo + n, 0:LANES] = x_v.astype(BF16)

    put(0, 0, SPAN, slice(0, SPAN), q_ref[...], k_ref[...], v_ref[...])
    d1, d2 = DILATIONS[1], DILATIONS[2] // DILATIONS[1]
    n1 = SPAN // d1
    for r in range(d1):
        xs = [ref[pl.ds(r, n1, stride=d1), :] for ref in (q_ref, k_ref, v_ref)]
        for t, x in enumerate(xs):
            stage_ref[t, r * n1:(r + 1) * n1, :] = x
        put(1, r, n1, slice(r * n1, (r + 1) * n1), *xs)
    n2 = n1 // d2
    for r in range(DILATIONS[2]):
        rows = pl.ds((r % d1) * n1 + r // d1, n2, stride=d2)
        xs = [stage_ref[t, rows, :] for t in range(3)]
        put(2, r, n2, slice(r * n2, (r + 1) * n2), *xs)

    qi = lax.broadcasted_iota(jnp.int32, (QBLOCK, 2 * QBLOCK), 0)
    kc = lax.broadcasted_iota(jnp.int32, (QBLOCK, 2 * QBLOCK), 1)
    band = (kc >= qi) & (kc <= qi + NEIGHBOURS)
    bias_ref[0] = jnp.where(band, 0.0, NEG_INF)
    bias_ref[1] = jnp.where(band & (kc >= QBLOCK), 0.0, NEG_INF)

    def key_rows(u, nb):
        return pl.ds((u + u // nb) * QBLOCK, 2 * QBLOCK)

    def scores(u):
        zero = jnp.zeros((QBLOCK, LANES), BF16)
        for g, d in enumerate(DILATIONS):
            nb, _ = _group_geometry(d)
            kb = kg[g][key_rows(u, nb), :]
            qb = qg[g][pl.ds(u * QBLOCK, QBLOCK), :]
            s_ref[u % 2, 2 * g] = _dot_nt(jnp.where(head0, qb, zero), kb)
            s_ref[u % 2, 2 * g + 1] = _dot_nt(jnp.where(head0, zero, qb), kb)

    def result_rows(u, g):
        if g == 2:
            return pl.ds((u % d1) * n1 + u // d1, QBLOCK, stride=d2)
        return pl.ds(u * QBLOCK, QBLOCK)

    def softmax(u):
        for g, d in enumerate(DILATIONS):
            nb, _ = _group_geometry(d)
            first = jnp.where(sp == 0, 1, 0) if u % nb == 0 else 0
            ms = []
            for h in range(2):
                s = s_ref[u % 2, 2 * g + h] + bias_ref[first]
                m = jnp.max(s, axis=-1, keepdims=True)
                p_ref[u % 2, 2 * g + h] = jnp.exp2(s - m).astype(BF16)
                ms.append(m)
            max_n[g][result_rows(u, g), :] = jnp.where(head0, ms[0], ms[1])

    def values(u):
        for g, d in enumerate(DILATIONS):
            nb, _ = _group_geometry(d)
            vb = vg[g][key_rows(u, nb), :]
            e0 = _dot(p_ref[u % 2, 2 * g], vb)
            e1 = _dot(p_ref[u % 2, 2 * g + 1], vb)
            rows = result_rows(u, g)
            acc_n[g][rows, :] = jnp.where(head0, e0[:, :LANES], e1[:, :LANES])
            den_n[g][rows, :] = jnp.where(head0, e0[:, LANES:], e1[:, LANES:])

    n_units = SPAN // QBLOCK
    for u in range(n_units + 2):
        if u >= 2:
            values(u - 2)
        if 1 <= u <= n_units:
            softmax(u - 1)
        if u < n_units:
            scores(u)

    def merge(c, carry):
        major = pl.ds(pl.multiple_of(c * QBLOCK, QBLOCK), QBLOCK)
        by_pos = pl.ds(lax.bitwise_and(c, d1 - 1) * (QBLOCK * d1) + lax.shift_right_logical(c, 2),
                       QBLOCK, stride=d1)
        rows = (by_pos, major, major)
        ms = [max_n[g][rows[g], :] for g in range(n_g)]
        m_all = functools.reduce(jnp.maximum, ms)
        ws = [jnp.exp2(m - m_all) for m in ms]
        num = sum(w * acc_n[g][rows[g], :] for g, w in enumerate(ws))
        den = sum(w * den_n[g][rows[g], :] for g, w in enumerate(ws))
        o_ref[by_pos, :] = num / den
        return carry

    lax.fori_loop(0, n_units, merge, 0)


def _attn_prompt(q, k, v):
    batch, seq, _ = q.shape
    blk = pl.BlockSpec((None, SPAN, LANES), lambda b, hp, s: (b, s, hp))
    n_g = len(DILATIONS)
    scratch = [pltpu.VMEM((SPAN, LANES), BF16) for _ in DILATIONS]
    scratch += [pltpu.VMEM((d * _group_geometry(d)[1], LANES), BF16) for d in DILATIONS]
    scratch += [pltpu.VMEM((d * _group_geometry(d)[1], 2 * LANES), BF16) for d in DILATIONS]
    scratch += [pltpu.VMEM((SPAN, LANES), F32) for _ in range(3 * n_g)]
    scratch += [pltpu.VMEM((3, SPAN, LANES), F32),
                pltpu.VMEM((2, 2 * n_g, QBLOCK, 2 * QBLOCK), F32),
                pltpu.VMEM((2, 2 * n_g, QBLOCK, 2 * QBLOCK), BF16),
                pltpu.VMEM((2, QBLOCK, 2 * QBLOCK), F32)]
    return pl.pallas_call(
        _attn_prompt_kernel,
        grid=(batch, D_ATTN // LANES, seq // SPAN),
        in_specs=[blk, blk, blk],
        out_specs=blk,
        out_shape=jax.ShapeDtypeStruct((batch, seq, D_ATTN), F32),
        scratch_shapes=scratch,
        compiler_params=_params(3),
        name="attn_prompt",
    )(q, k, v)


def _group_count(dist):
    total = None
    for d in DILATIONS:
        sh = d.bit_length() - 1
        hit = ((dist & (d - 1)) == 0) & ((dist >> sh) <= NEIGHBOURS) & (dist >= 0)
        total = hit.astype(F32) if total is None else total + hit.astype(F32)
    return total


SAMPLE_NEW_SHAPE = (N_HEADS, DEC_SEQ, HEAD_DIM)
SAMPLE_CACHE_SHAPE = (N_HEADS, HEAD_DIM, CACHE_LEN)
SAMPLE_PC_SHAPE = (N_HEADS, DEC_SEQ, CACHE_LEN)
SAMPLE_PN_SHAPE = (N_HEADS, DEC_SEQ, DEC_SEQ)


def _sample_probs(q_ref, kn_ref, kc_ref, pc_ref, pn_ref, first_head):
    qi = lax.broadcasted_iota(jnp.int32, (DEC_SEQ, CACHE_LEN), 0)
    pos = lax.broadcasted_iota(jnp.int32, (DEC_SEQ, CACHE_LEN), 1)
    cnt_c = _group_count(CACHE_LEN + qi - pos)
    qn = lax.broadcasted_iota(jnp.int32, (DEC_SEQ, DEC_SEQ), 0)
    kn = lax.broadcasted_iota(jnp.int32, (DEC_SEQ, DEC_SEQ), 1)
    cnt_n = _group_count(qn - kn)
    for i in range(kc_ref.shape[0]):
        h = first_head + i
        qh = (q_ref[h] * SCALE_LOG2E).astype(BF16)
        s_c = jnp.where(cnt_c > 0, _dot(qh, kc_ref[i].astype(BF16)), NEG_INF)
        s_n = jnp.where(cnt_n > 0, _dot_nt(qh, kn_ref[h].astype(BF16)), NEG_INF)
        m = jnp.maximum(jnp.max(s_c, axis=-1, keepdims=True), jnp.max(s_n, axis=-1, keepdims=True))
        w_c = cnt_c * jnp.exp2(s_c - m)
        w_n = cnt_n * jnp.exp2(s_n - m)
        den = jnp.sum(w_c, axis=-1, keepdims=True) + jnp.sum(w_n, axis=-1, keepdims=True)
        pc_ref[i] = (w_c / den).astype(BF16)
        pn_ref[i] = w_n / den


def _sample_values(pc_ref, pn_ref, vc_ref, vn_ref, o_ref, first_head):
    for i in range(pc_ref.shape[0]):
        h = first_head + i
        o_ref[h] = (_dot_nt(pc_ref[i], vc_ref[h].astype(BF16))
                    + _dot(pn_ref[i].astype(BF16), vn_ref[h].astype(BF16)))


def _per_sequence_spec(shape, steps_per_batch):
    zeros = (0,) * len(shape)
    return pl.BlockSpec((None,) + shape, lambda b, j: (b * steps_per_batch + j,) + zeros)


FFN_HALO = 8


def _mix_ffn(x_ref, a_ref, o_ref, wo_ref, g2_ref, wg_ref, wu_ref, cw_ref, wd_ref, gf_ref, y_ref,
             x1_ref, h_ref, act_ref, conv):
    x1 = x_ref[...] + _dot(a_ref[...].astype(BF16), wo_ref[0:D_CONV, :])
    x1 = x1 + _dot(o_ref[...].astype(BF16), wo_ref[D_CONV:, :])
    x1_ref[...] = x1
    h_ref[...] = _rms_norm(x1, g2_ref[...]).astype(BF16)
    for c in range(N_FF_CHUNKS):
        cols = slice(c * FF_CHUNK, (c + 1) * FF_CHUNK)
        h = h_ref[...]
        g = conv(c, cols, _dot(h, wg_ref[:, cols]), cw_ref[:, cols])
        act_ref[:, cols] = ((g * jax.nn.sigmoid(g)) * _dot(h, wu_ref[:, cols])).astype(BF16)
    y_ref[...] = _rms_norm(x1_ref[...] + _dot(act_ref[...], wd_ref[...]), gf_ref[...])


def _conv_taps(cw, now, back1, back2):
    return cw[2:3, :] * now + cw[1:2, :] * back1 + cw[0:1, :] * back2 + cw[3:4, :]


def _ffn_prompt_kernel(x_ref, a_ref, o_ref, wo_ref, g2_ref, wg_ref, wu_ref, cw_ref, wd_ref, gf_ref,
                       sq_ref, skn_ref, skc_ref, svc_ref, svn_ref,
                       y_ref, st_ref, so_ref, x1_ref, h_ref, act_ref, gbuf_ref, carry_ref,
                       spc_ref, spn_ref):
    @pl.when(pl.program_id(1) == 0)
    def _():
        carry_ref[...] = jnp.zeros(carry_ref.shape, F32)

    def conv(c, cols, gp, cw):
        buf = gbuf_ref.at[c % 2]
        buf[0:FFN_HALO, :] = carry_ref[:, cols]
        buf[FFN_HALO:, :] = gp
        tail = gp[ROW_TILE - FFN_HALO:, :]
        carry_ref[:, cols] = tail
        st_ref[:, cols] = tail
        return _conv_taps(cw, gp, buf[FFN_HALO - 1:FFN_HALO - 1 + ROW_TILE, :],
                          buf[FFN_HALO - 2:FFN_HALO - 2 + ROW_TILE, :])

    _sample_probs(sq_ref, skn_ref, skc_ref, spc_ref, spn_ref, first_head=0)
    _mix_ffn(x_ref, a_ref, o_ref, wo_ref, g2_ref, wg_ref, wu_ref, cw_ref, wd_ref, gf_ref, y_ref,
             x1_ref, h_ref, act_ref, conv)
    _sample_values(spc_ref, spn_ref, svc_ref, svn_ref, so_ref, first_head=0)


def _ffn_weight_specs():
    return [_const_spec((D_MODEL, D_MODEL)), _const_spec((1, D_MODEL)),
            _const_spec((D_MODEL, D_FF)), _const_spec((D_MODEL, D_FF)),
            _const_spec((8, D_FF)), _const_spec((D_FF, D_MODEL)), _const_spec((1, D_MODEL))]


def _ffn_scratch(rows):
    return [pltpu.VMEM((rows, D_MODEL), F32), pltpu.VMEM((rows, D_MODEL), BF16),
            pltpu.VMEM((rows, D_FF), BF16)]


def _ffn_prompt(x, a, o, wo, g2, wg, wu, cw, wd, gf, q4, kn4, kc_t, vc_t, vn4):
    batch, seq, _ = x.shape
    nseq = q4.shape[0]
    steps = seq // ROW_TILE
    row = lambda width: pl.BlockSpec((None, ROW_TILE, width), lambda b, j: (b, j, 0))
    st = pl.BlockSpec((None, FFN_HALO, D_FF), lambda b, j: (b, 0, 0))
    new = _per_sequence_spec(SAMPLE_NEW_SHAPE, steps)
    cache = _per_sequence_spec(SAMPLE_CACHE_SHAPE, steps)
    return pl.pallas_call(
        _ffn_prompt_kernel,
        grid=(batch, steps),
        in_specs=[row(D_MODEL), row(D_CONV), row(D_ATTN)] + _ffn_weight_specs()
        + [new, new, cache, cache, new],
        out_specs=[row(D_MODEL), st, new],
        out_shape=[jax.ShapeDtypeStruct((batch, seq, D_MODEL), F32),
                   jax.ShapeDtypeStruct((batch, FFN_HALO, D_FF), F32),
                   jax.ShapeDtypeStruct((nseq,) + SAMPLE_NEW_SHAPE, F32)],
        scratch_shapes=_ffn_scratch(ROW_TILE)
        + [pltpu.VMEM((2, FFN_HALO + ROW_TILE, FF_CHUNK), F32), pltpu.VMEM((FFN_HALO, D_FF), F32),
           pltpu.VMEM(SAMPLE_PC_SHAPE, BF16), pltpu.VMEM(SAMPLE_PN_SHAPE, F32)],
        compiler_params=_params(2),
        name="ffn_prompt",
    )(x, a, o, wo, g2, wg, wu, cw, wd, gf, q4, kn4, kc_t, vc_t, vn4)


def _ffn_sample_kernel(x_ref, a_ref, o_ref, past_ref, wo_ref, g2_ref, wg_ref, wu_ref, cw_ref, wd_ref,
                       gf_ref, y_ref, st_ref, x1_ref, h_ref, act_ref, gbuf_ref):
    rows = x_ref.shape[0]
    nseq = rows // DEC_SEQ

    def conv(c, cols, gp, cw):
        buf = gbuf_ref.at[c % 2]
        buf[0:2 * nseq, :] = past_ref[:, cols]
        buf[2 * nseq:, :] = gp
        st_ref[:, cols] = gp[rows - 2 * nseq:, :]
        return _conv_taps(cw, gp, buf[nseq:nseq + rows, :], buf[0:rows, :])

    _mix_ffn(x_ref, a_ref, o_ref, wo_ref, g2_ref, wg_ref, wu_ref, cw_ref, wd_ref, gf_ref, y_ref,
             x1_ref, h_ref, act_ref, conv)


def _ffn_sample(x, a, o, past, wo, g2, wg, wu, cw, wd, gf):
    rows = x.shape[0]
    nseq = rows // DEC_SEQ
    return pl.pallas_call(
        _ffn_sample_kernel,
        out_shape=[jax.ShapeDtypeStruct((rows, D_MODEL), F32),
                   jax.ShapeDtypeStruct((2 * nseq, D_FF), F32)],
        scratch_shapes=_ffn_scratch(rows) + [pltpu.VMEM((2, 2 * nseq + rows, FF_CHUNK), F32)],
        compiler_params=pltpu.CompilerParams(vmem_limit_bytes=VMEM_LIMIT),
        name="ffn_sample",
    )(x, a, o, past, wo, g2, wg, wu, cw, wd, gf)


def kernel(x_prompt, x_sample, cache_k_win, cache_v_win, state_conv_a, state_conv_ffn, norm_mix_g, w_in, conv_a_w, conv_a_b, ln_a_g, ln_a_b, w_out, norm_ffn_g, w_ffn_gate, w_ffn_up, ffn_conv_w, ffn_conv_b, w_ffn_down, norm_final_g):
    batch, seq, _ = x_prompt.shape
    nseq, dec_seq, _ = x_sample.shape
    assert w_in.shape[0] == 1, "single trunk layer"
    assert dec_seq == DEC_SEQ and cache_k_win.shape[2] == CACHE_LEN
    assert seq % SPAN == 0 and seq >= CACHE_LEN and (nseq * dec_seq) % ROW_TILE == 0
    assert batch * (seq // ROW_TILE) == nseq, "one sample sequence per prompt row tile"

    g1 = norm_mix_g
    w_in_b = w_in[0].astype(BF16)
    wkt_b = w_in_b[:, K_COLS[0]:K_COLS[1]].T
    wvt_b = w_in_b[:, V_COLS[0]:V_COLS[1]].T
    ca = (conv_a_w[0], conv_a_b, ln_a_g, ln_a_b)
    cw = jnp.concatenate([ffn_conv_w[0], ffn_conv_b, jnp.zeros((4, D_FF), F32)], axis=0)
    ffn_w = (w_out[0].astype(BF16), norm_ffn_g, w_ffn_gate[0].astype(BF16), w_ffn_up[0].astype(BF16),
             cw, w_ffn_down[0].astype(BF16), norm_final_g[None, :])

    rows = dec_seq * nseq
    xs = x_sample.transpose(1, 0, 2).reshape(rows, D_MODEL)
    glu_s, q_s, k_s, v_s = _in_proj_sample(xs, g1, w_in_b)
    a_s, st_a = _conv_a_sample(state_conv_a[0].transpose(1, 0, 2),
                               glu_s.reshape(dec_seq, nseq, D_CONV), *ca)
    per_head = lambda t: t.reshape(dec_seq, nseq, N_HEADS, HEAD_DIM).transpose(1, 2, 0, 3)
    cache_t = lambda t: t[0].transpose(0, 2, 3, 1)

    a_p, q_p, k_p, v_p, kt_win, vt_win, glu_tail = _in_proj_prompt(
        x_prompt, g1, w_in_b, wkt_b, wvt_b, *ca)
    o_p = _attn_prompt(q_p, k_p, v_p)
    y_p, st_p, o4 = _ffn_prompt(x_prompt, a_p, o_p, *ffn_w, per_head(q_s), per_head(k_s),
                                cache_t(cache_k_win), cache_t(cache_v_win), per_head(v_s))

    window = lambda t: t.reshape(batch, N_HEADS, HEAD_DIM, CACHE_LEN).transpose(0, 3, 1, 2)[None]
    conv_a_prompt = glu_tail[:, CONV_HALO - (CONV_A_WIDTH - 1):][None]
    conv_ffn_prompt = st_p[:, FFN_HALO - 2:][None]

    o_s = o4.transpose(2, 0, 1, 3).reshape(rows, D_ATTN)
    past_f = state_conv_ffn[0].transpose(1, 0, 2).reshape(2 * nseq, D_FF)
    y_s, st_f = _ffn_sample(xs, a_s.reshape(rows, D_CONV), o_s, past_f, *ffn_w)

    y_sample = y_s.reshape(dec_seq, nseq, D_MODEL).transpose(1, 0, 2)
    new_rows = lambda t: t.reshape(dec_seq, nseq, N_HEADS, HEAD_DIM).transpose(1, 0, 2, 3)[None]
    conv_a_sample = st_a.transpose(1, 0, 2)[None]
    conv_ffn_sample = st_f.reshape(2, nseq, D_FF).transpose(1, 0, 2)[None]

    return (y_p, y_sample, window(kt_win), window(vt_win), conv_a_prompt, conv_ffn_prompt,
            new_rows(k_s), new_rows(v_s), conv_a_sample, conv_ffn_sample)
```
